```python
import math
import jax, jax.numpy as jnp
from jax import lax
import numpy as np

D_MODEL = 4096
BATCH = 4
SEQ = 2048
DEPTH = 2
DEC_BATCH = 8
DEC_SEQ = 1
PAST_LEN = 16384
PAGE_SIZE = 128

N_MIXERS = 2
N_ATT_LAYERS = (DEPTH + 1) // 2
N_REC_LAYERS = DEPTH // 2
ATT_HEADS = 32
ATT_HEAD_DIM = D_MODEL // ATT_HEADS
Q_BLOCK = 128
ATT_FORGET_BIAS_INIT = 3.0
REC_EXPAND = 128
REC_HEADS = D_MODEL // REC_EXPAND
REC_KDIM = REC_EXPAND
REC_VDIM = D_MODEL // REC_HEADS
REC_CHUNK = 64
N_GROUPS = 8
EXPERTS_PER_GROUP = 4
N_EXPERTS = N_GROUPS * EXPERTS_PER_GROUP
TOP_K = 2
D_EXPERT = D_MODEL // 4
MOE_MAX_BLOCK = 256
MOE_MIN_BLOCK = 8
EPS = 1e-6

kernel_name = 'fox_hgrn2_grouped_moe_adaln_step'

F32 = jnp.float32


def rms_norm(x, g):
    xf = x.astype(F32)
    y = xf * lax.rsqrt(jnp.mean(xf * xf, axis=-1, keepdims=True) + EPS)
    return (y * g.astype(F32)).astype(x.dtype)


def modulate(x, silu_c, g, w, b):
    m = jnp.dot(silu_c, w) + b
    shift, scale, gate = jnp.split(m, 3, axis=-1)
    h = rms_norm(x, g) * (1 + scale[:, None, :]) + shift[:, None, :]
    return h, gate[:, None, :]


def fox_project(h, w_in, b_f, q_g, k_g):
    B, L, _ = h.shape
    z = jnp.einsum('bld,de->ble', h, w_in)
    q, k, v, f = jnp.split(z, [D_MODEL, 2 * D_MODEL, 3 * D_MODEL], axis=-1)
    q = rms_norm(q.reshape(B, L, ATT_HEADS, ATT_HEAD_DIM), q_g)
    k = rms_norm(k.reshape(B, L, ATT_HEADS, ATT_HEAD_DIM), k_g)
    v = v.reshape(B, L, ATT_HEADS, ATT_HEAD_DIM)
    logf = jax.nn.log_sigmoid((f + b_f).astype(F32))
    return q, k, v, logf


def fox_prompt(q, k, v, logf):
    B, L, H, HD = q.shape
    scale = HD ** -0.5
    cum = jnp.cumsum(logf, axis=1).transpose(0, 2, 1)
    kpos = jnp.arange(L)

    def block(i):
        start = i * Q_BLOCK
        qb = lax.dynamic_slice_in_dim(q, start, Q_BLOCK, axis=1)
        cb = lax.dynamic_slice_in_dim(cum, start, Q_BLOCK, axis=2)
        s = jnp.einsum('bqhd,bkhd->bhqk', qb, k, preferred_element_type=F32) * scale
        s = s + (cb[..., :, None] - cum[..., None, :])
        qpos = start + jnp.arange(Q_BLOCK)
        s = jnp.where(kpos[None, :] <= qpos[:, None], s, -jnp.inf)
        p = jax.nn.softmax(s, axis=-1)
        return jnp.einsum('bhqk,bkhd->bqhd', p.astype(v.dtype), v)

    o = lax.map(block, jnp.arange(L // Q_BLOCK))
    return o.transpose(1, 0, 2, 3, 4).reshape(B, L, H * HD)


def fox_sample(q, k, v, logf, ck, cv, clogf, page_table):
    DB, T, H, HD = q.shape
    n_pages = page_table.shape[1]
    scale = HD ** -0.5
    past_logf = clogf[page_table].reshape(DB, n_pages * PAGE_SIZE, H).astype(F32)
    tail = lax.cumsum(past_logf, axis=1, reverse=True) - past_logf
    tail_pages = tail.reshape(DB, n_pages, PAGE_SIZE, H).transpose(1, 0, 3, 2)
    cn = jnp.cumsum(logf, axis=1).transpose(0, 2, 1)

    def step(carry, xs):
        m, l, acc = carry
        pid, tl = xs
        kp = ck[pid]
        vp = cv[pid]
        s = jnp.einsum('bqhd,bkhd->bhqk', q, kp, preferred_element_type=F32) * scale
        s = s + cn[..., :, None] + tl[:, :, None, :]
        m_new = jnp.maximum(m, jnp.max(s, axis=-1))
        a = jnp.exp(m - m_new)
        p = jnp.exp(s - m_new[..., None])
        l = l * a + jnp.sum(p, axis=-1)
        acc = acc * a[..., None] + jnp.einsum('bhqk,bkhd->bhqd', p, vp.astype(F32))
        return (m_new, l, acc), None

    init = (jnp.full((DB, H, T), -jnp.inf, F32), jnp.zeros((DB, H, T), F32),
            jnp.zeros((DB, H, T, HD), F32))
    (m, l, acc), _ = lax.scan(step, init, (page_table.T, tail_pages))

    s = jnp.einsum('bqhd,bkhd->bhqk', q, k, preferred_element_type=F32) * scale
    s = s + (cn[..., :, None] - cn[..., None, :])
    causal = jnp.tril(jnp.ones((T, T), bool))
    s = jnp.where(causal, s, -jnp.inf)
    m_new = jnp.maximum(m, jnp.max(s, axis=-1))
    a = jnp.exp(m - m_new)
    p = jnp.exp(s - m_new[..., None])
    l = l * a + jnp.sum(p, axis=-1)
    acc = acc * a[..., None] + jnp.einsum('bhqk,bkhd->bhqd', p, v.astype(F32))
    o = (acc / l[..., None]).transpose(0, 2, 1, 3).reshape(DB, T, H * HD)
    return o.astype(q.dtype)


def hgrn2_scan(q, k, v, logf, s0):
    B, L, H, DK = q.shape
    DV = v.shape[-1]
    C = min(REC_CHUNK, L)
    n = -(-L // C)
    pad = n * C - L
    if pad:
        padw = ((0, 0), (0, pad), (0, 0), (0, 0))
        q, k, v, logf = (jnp.pad(t, padw) for t in (q, k, v, logf))

    def chunks(t):
        return t.reshape(B, n, C, H, t.shape[-1]).transpose(1, 0, 3, 2, 4)

    mask = jnp.tril(jnp.ones((C, C), bool))[:, :, None]

    def step(S, xs):
        qc, kc, vc, gc = xs
        b = jnp.cumsum(gc, axis=2)
        o_inter = jnp.einsum('bhtk,bhkv->bhtv', qc * jnp.exp(b), S)
        diff = jnp.where(mask, b[:, :, :, None, :] - b[:, :, None, :, :], -jnp.inf)
        A = jnp.einsum('bhtk,bhtsk->bhts', qc, jnp.exp(diff) * kc[:, :, None, :, :])
        o_intra = jnp.einsum('bhts,bhsv->bhtv', A, vc)
        bl = b[:, :, -1:, :]
        S_new = jnp.exp(bl[:, :, 0, :])[..., None] * S + jnp.einsum('bhsk,bhsv->bhkv', kc * jnp.exp(bl - b), vc)
        return S_new, o_inter + o_intra

    S, o = lax.scan(step, s0.astype(F32), (chunks(q), chunks(k), chunks(v), chunks(logf)))
    o = o.transpose(1, 0, 3, 2, 4).reshape(B, n * C, H, DV)[:, :L]
    return o, S


def hgrn2_mixer(h, w_in, lb, o_norm, w_o, s0):
    B, L, _ = h.shape
    z = jnp.einsum('bld,de->ble', h, w_in)
    q, f, i, g = jnp.split(z, 4, axis=-1)
    f = f.astype(F32)
    lbf = lb.astype(F32)
    log_forget = jnp.logaddexp(jnp.log(lbf), jnp.log1p(-lbf) + jax.nn.log_sigmoid(f))
    k = (1 - lbf) * jax.nn.sigmoid(-f)
    shp_k = (B, L, REC_HEADS, REC_KDIM)
    q = jax.nn.silu(q.astype(F32)).reshape(shp_k)
    o, S = hgrn2_scan(q, k.reshape(shp_k), i.astype(F32).reshape(B, L, REC_HEADS, REC_VDIM),
                      log_forget.reshape(shp_k), s0)
    o = rms_norm(o, o_norm).reshape(B, L, D_MODEL) * jax.nn.silu(g.astype(F32))
    out = jnp.einsum('bld,de->ble', o.astype(h.dtype), w_o)
    return out, S.astype(s0.dtype)


def route(x2d, router_w, router_b):
    T = x2d.shape[0]
    scores = jax.nn.sigmoid(jnp.dot(x2d, router_w, preferred_element_type=F32))
    sel = (scores + router_b.astype(F32)).reshape(T, N_GROUPS, EXPERTS_PER_GROUP)
    group_score = jnp.sum(lax.top_k(sel, TOP_K)[0], axis=-1)
    g_idx = jnp.argmax(group_score, axis=-1)
    cand = jnp.take_along_axis(sel, g_idx[:, None, None], axis=1)[:, 0]
    _, local = lax.top_k(cand, TOP_K)
    e_idx = g_idx[:, None] * EXPERTS_PER_GROUP + local
    w = jnp.take_along_axis(scores, e_idx, axis=1)
    return e_idx, w / jnp.sum(w, axis=-1, keepdims=True)


def moe_ffn(h, router_w, router_b, w_gate, w_up, w_down):
    B, L, D = h.shape
    x = h.reshape(-1, D)
    T = x.shape[0]
    e_idx, gate = route(x, router_w, router_b)
    A = T * TOP_K
    e_flat = e_idx.reshape(-1)
    g_flat = gate.reshape(-1)
    tok = jnp.arange(A) // TOP_K
    blk = max(MOE_MIN_BLOCK, min(MOE_MAX_BLOCK, A // N_EXPERTS))
    cap = -(-(A + N_EXPERTS * (blk - 1)) // blk) * blk
    n_blocks = cap // blk
    order = jnp.argsort(e_flat)
    e_sorted = e_flat[order]
    counts = jnp.bincount(e_flat, length=N_EXPERTS)
    padded = (counts + blk - 1) // blk * blk
    pad_end = jnp.cumsum(padded)
    pad_start = pad_end - padded
    start = jnp.cumsum(counts) - counts
    dest = pad_start[e_sorted] + jnp.arange(A) - start[e_sorted]
    buf_tok = jnp.full((cap,), T, jnp.int32).at[dest].set(tok[order].astype(jnp.int32))
    buf_gate = jnp.zeros((cap,), F32).at[dest].set(g_flat[order])
    block_e = jnp.minimum(jnp.searchsorted(pad_end, jnp.arange(n_blocks) * blk, side='right'), N_EXPERTS - 1)
    xp = jnp.concatenate([x, jnp.zeros((1, D), x.dtype)], axis=0)
    xb = xp[buf_tok].reshape(n_blocks, blk, D)

    def expert_block(args):
        xe, e = args
        a = jax.nn.silu(jnp.dot(xe, w_gate[e])) * jnp.dot(xe, w_up[e])
        return jnp.dot(a, w_down[e])

    yb = lax.map(expert_block, (xb, block_e)).reshape(cap, D)
    y = jnp.zeros((T + 1, D), F32).at[buf_tok].add(yb.astype(F32) * buf_gate[:, None])
    return y[:T].astype(h.dtype).reshape(B, L, D)


def setup_inputs(seed: int = 0) -> dict:
    key = jax.random.key(seed)
    ks = jax.random.split(key, 32)
    n_pages = PAST_LEN // PAGE_SIZE
    n_pool = (5 * DEC_BATCH * n_pages + 3) // 4

    def nrm(k, shape, s):
        return jax.random.normal(k, shape, F32) * s

    ds = D_MODEL ** -0.5
    page_table = jax.random.permutation(ks[8], n_pool)[:DEC_BATCH * n_pages].reshape(DEC_BATCH, n_pages).astype(jnp.int32)
    return {
        'x_prompt': nrm(ks[0], (BATCH, SEQ, D_MODEL), 1.0),
        'x_sample': nrm(ks[1], (DEC_BATCH, DEC_SEQ, D_MODEL), 1.0),
        'c_prompt': nrm(ks[2], (BATCH, D_MODEL), 1.0),
        'c_sample': nrm(ks[3], (DEC_BATCH, D_MODEL), 1.0),
        'cache_k': nrm(ks[4], (N_ATT_LAYERS, n_pool, PAGE_SIZE, ATT_HEADS, ATT_HEAD_DIM), 1.0),
        'cache_v': nrm(ks[5], (N_ATT_LAYERS, n_pool, PAGE_SIZE, ATT_HEADS, ATT_HEAD_DIM), 1.0),
        'cache_logf': jax.nn.log_sigmoid(nrm(ks[6], (N_ATT_LAYERS, n_pool, PAGE_SIZE, ATT_HEADS), 1.0) + ATT_FORGET_BIAS_INIT),
        'state_hgrn': nrm(ks[7], (N_REC_LAYERS, DEC_BATCH, REC_HEADS, REC_KDIM, REC_VDIM), 0.5),
        'page_table': page_table,
        'ada_w': nrm(ks[9], (DEPTH, 2, D_MODEL, 3 * D_MODEL), 0.5 * ds),
        'ada_b': nrm(ks[10], (DEPTH, 2, 3 * D_MODEL), 0.02),
        'norm_g': 1.0 + nrm(ks[11], (DEPTH, 2, D_MODEL), 0.02),
        'att_w_in': nrm(ks[12], (N_ATT_LAYERS, D_MODEL, 3 * D_MODEL + ATT_HEADS), ds),
        'att_b_f': ATT_FORGET_BIAS_INIT + nrm(ks[13], (N_ATT_LAYERS, ATT_HEADS), 0.1),
        'att_q_norm': 1.0 + nrm(ks[14], (N_ATT_LAYERS, ATT_HEAD_DIM), 0.02),
        'att_k_norm': 1.0 + nrm(ks[15], (N_ATT_LAYERS, ATT_HEAD_DIM), 0.02),
        'att_w_o': nrm(ks[16], (N_ATT_LAYERS, D_MODEL, D_MODEL), ds),
        'rec_w_in': nrm(ks[17], (N_REC_LAYERS, D_MODEL, 4 * D_MODEL), ds),
        'rec_lb_logits': nrm(ks[18], (DEPTH, D_MODEL), 0.5),
        'rec_o_norm': 1.0 + nrm(ks[19], (N_REC_LAYERS, REC_VDIM), 0.02),
        'rec_w_o': nrm(ks[20], (N_REC_LAYERS, D_MODEL, D_MODEL), ds),
        'router_w': nrm(ks[21], (D_MODEL, N_EXPERTS), ds),
        'router_b': nrm(ks[22], (N_EXPERTS,), 0.01),
        'moe_w_gate': nrm(ks[23], (DEPTH, N_EXPERTS, D_MODEL, D_EXPERT), ds),
        'moe_w_up': nrm(ks[24], (DEPTH, N_EXPERTS, D_MODEL, D_EXPERT), ds),
        'moe_w_down': nrm(ks[25], (DEPTH, N_EXPERTS, D_EXPERT, D_MODEL), D_EXPERT ** -0.5),
    }


def reference(x_prompt, x_sample, c_prompt, c_sample, cache_k, cache_v, cache_logf, state_hgrn, page_table,
              ada_w, ada_b, norm_g, att_w_in, att_b_f, att_q_norm, att_k_norm, att_w_o,
              rec_w_in, rec_lb_logits, rec_o_norm, rec_w_o, router_w, router_b,
              moe_w_gate, moe_w_up, moe_w_down):
    p_lb = jax.nn.softmax(rec_lb_logits.astype(F32), axis=0)
    lb_table = jnp.cumsum(p_lb, axis=0) - p_lb[0]
    silu_cp = jax.nn.silu(c_prompt)
    silu_cs = jax.nn.silu(c_sample)
    xp, xs = x_prompt, x_sample
    kp_l, vp_l, lfp_l, ks_l, vs_l, lfs_l, sp_l, ss_l = [], [], [], [], [], [], [], []
    for layer in range(DEPTH):
        j = layer // N_MIXERS
        hp, gp = modulate(xp, silu_cp, norm_g[layer, 0], ada_w[layer, 0], ada_b[layer, 0])
        hs, gs = modulate(xs, silu_cs, norm_g[layer, 0], ada_w[layer, 0], ada_b[layer, 0])
        if layer % N_MIXERS == 0:
            qp, kp, vp, lfp = fox_project(hp, att_w_in[j], att_b_f[j], att_q_norm[j], att_k_norm[j])
            qs, kq, vq, lfs = fox_project(hs, att_w_in[j], att_b_f[j], att_q_norm[j], att_k_norm[j])
            op = fox_prompt(qp, kp, vp, lfp)
            osm = fox_sample(qs, kq, vq, lfs, cache_k[j], cache_v[j], cache_logf[j], page_table)
            mp = jnp.einsum('bld,de->ble', op, att_w_o[j])
            ms = jnp.einsum('bld,de->ble', osm, att_w_o[j])
            kp_l.append(kp); vp_l.append(vp); lfp_l.append(lfp)
            ks_l.append(kq); vs_l.append(vq); lfs_l.append(lfs)
        else:
            s0p = jnp.zeros((xp.shape[0], REC_HEADS, REC_KDIM, REC_VDIM), state_hgrn.dtype)
            mp, sp = hgrn2_mixer(hp, rec_w_in[j], lb_table[layer], rec_o_norm[j], rec_w_o[j], s0p)
            ms, ss = hgrn2_mixer(hs, rec_w_in[j], lb_table[layer], rec_o_norm[j], rec_w_o[j], state_hgrn[j])
            sp_l.append(sp); ss_l.append(ss)
        xp = xp + gp * mp
        xs = xs + gs * ms
        hp, gp = modulate(xp, silu_cp, norm_g[layer, 1], ada_w[layer, 1], ada_b[layer, 1])
        hs, gs = modulate(xs, silu_cs, norm_g[layer, 1], ada_w[layer, 1], ada_b[layer, 1])
        xp = xp + gp * moe_ffn(hp, router_w, router_b, moe_w_gate[layer], moe_w_up[layer], moe_w_down[layer])
        xs = xs + gs * moe_ffn(hs, router_w, router_b, moe_w_gate[layer], moe_w_up[layer], moe_w_down[layer])
    return (xp, xs, jnp.stack(kp_l), jnp.stack(vp_l), jnp.stack(lfp_l), jnp.stack(ks_l), jnp.stack(vs_l),
            jnp.stack(lfs_l), jnp.stack(sp_l), jnp.stack(ss_l))
```

```python
import functools

import jax
import jax.numpy as jnp
from jax import lax
from jax.experimental import pallas as pl
from jax.experimental.pallas import tpu as pltpu

F32 = jnp.float32
BF16 = jnp.bfloat16
I32 = jnp.int32
EPS = 1e-6

V7X_LANES = 128
V7X_SUBLANES = 8
V7X_VMEM_LIMIT_BYTES = 56 * 1024 * 1024

ATT_HEADS = 32
REC_HEADS = 32
N_GROUPS = 8
EXPERTS_PER_GROUP = 4
N_EXPERTS = N_GROUPS * EXPERTS_PER_GROUP
TOP_K = 2
REC_SUB = 16
NEG_INF = float("-inf")


def _params(sem):
    return pltpu.CompilerParams(dimension_semantics=sem, vmem_limit_bytes=V7X_VMEM_LIMIT_BYTES)


def _split3(x):
    x1 = x.astype(BF16)
    r1 = x - x1.astype(F32)
    x2 = r1.astype(BF16)
    x3 = (r1 - x2.astype(F32)).astype(BF16)
    return x1, x2, x3


def _dot01(m01, x):
    x1, x2, x3 = _split3(x)
    d = lambda b: jnp.dot(m01, b, preferred_element_type=F32)
    return d(x1) + (d(x2) + d(x3))


def _iota01(shape, pred):
    r = lax.broadcasted_iota(I32, shape, 0)
    c = lax.broadcasted_iota(I32, shape, 1)
    return jnp.where(pred(r, c), 1.0, 0.0).astype(BF16)


def _sigmoid(x):
    return 1.0 / (1.0 + jnp.exp(-x))


def _silu(x):
    return x * _sigmoid(x)


def _log_sigmoid(x):
    return jnp.minimum(x, 0.0) - jnp.log1p(jnp.exp(-jnp.abs(x)))


def _logaddexp(a, b):
    return jnp.maximum(a, b) + jnp.log1p(jnp.exp(-jnp.abs(a - b)))


def _ada_kernel(c_ref, w_ref, b_ref, o_ref):
    sc = _silu(c_ref[...]).astype(BF16)
    o_ref[...] = jnp.dot(sc, w_ref[...].astype(BF16), preferred_element_type=F32) + b_ref[...]


def ada_modulation(c_all, ada_w, ada_b, tn=512):
    S, D, N = ada_w.shape
    R = c_all.shape[0]
    tn = min(tn, N)
    return pl.pallas_call(
        _ada_kernel,
        grid=(S, N // tn),
        in_specs=[pl.BlockSpec((R, D), lambda s, j: (0, 0)),
                  pl.BlockSpec((None, D, tn), lambda s, j: (s, 0, j)),
                  pl.BlockSpec((None, 1, tn), lambda s, j: (s, 0, j))],
        out_specs=pl.BlockSpec((None, R, tn), lambda s, j: (s, 0, j)),
        out_shape=jax.ShapeDtypeStruct((S, R, N), F32),
        compiler_params=_params(("parallel", "parallel")),
        name="ada_modulation",
    )(c_all, ada_w, ada_b)


def _norm_mod_kernel(x_ref, g_ref, scale_ref, shift_ref, o_ref):
    x = x_ref[...]
    y = x * lax.rsqrt(jnp.mean(x * x, axis=-1, keepdims=True) + EPS)
    h = (y * g_ref[...]) * (1.0 + scale_ref[...]) + shift_ref[...]
    o_ref[...] = h.astype(o_ref.dtype)


def norm_modulate(x, g, scale, shift, out_dtype=BF16, tr=256):
    NB, L, D = x.shape
    LS = scale.shape[1]
    tr = min(tr, L)
    ts = tr if LS == L else 1
    smap = (lambda b, i: (b, i, 0)) if LS == L else (lambda b, i: (b, 0, 0))
    return pl.pallas_call(
        _norm_mod_kernel,
        grid=(NB, L // tr),
        in_specs=[pl.BlockSpec((None, tr, D), lambda b, i: (b, i, 0)),
                  pl.BlockSpec((1, D), lambda b, i: (0, 0)),
                  pl.BlockSpec((None, ts, D), smap),
                  pl.BlockSpec((None, ts, D), smap)],
        out_specs=pl.BlockSpec((None, tr, D), lambda b, i: (b, i, 0)),
        out_shape=jax.ShapeDtypeStruct((NB, L, D), out_dtype),
        compiler_params=_params(("parallel", "parallel")),
        name="norm_modulate",
    )(x, g, scale, shift)


def _head_rms(acc, g):
    parts = []
    for c in range(acc.shape[1] // V7X_LANES):
        blk = acc[:, c * V7X_LANES:(c + 1) * V7X_LANES]
        y = blk * lax.rsqrt(jnp.mean(blk * blk, axis=-1, keepdims=True) + EPS)
        parts.append(y * g)
    return jnp.concatenate(parts, axis=-1) if len(parts) > 1 else parts[0]


def _proj_kernel(*refs, mode):
    a_ref, w_ref = refs[0], refs[1]
    wbf_ref = refs[-1]

    @pl.when(pl.program_id(1) == 0)
    def _():
        wbf_ref[...] = w_ref[...].astype(BF16)

    acc = jnp.dot(a_ref[...], wbf_ref[...], preferred_element_type=F32)
    if mode == "plain":
        refs[2][...] = acc
    elif mode == "headnorm":
        y = _head_rms(acc, refs[2][...])
        refs[3][...] = y
        refs[4][...] = y.astype(BF16)
    elif mode == "headnorm_bf":
        refs[3][...] = _head_rms(acc, refs[2][...]).astype(BF16)
    elif mode == "dual":
        refs[2][...] = acc
        refs[3][...] = acc.astype(BF16)
    elif mode == "logsig":
        refs[3][...] = _log_sigmoid(acc + refs[2][...])
    elif mode == "resid":
        refs[4][...] = refs[2][...] + refs[3][...] * acc
    else:
        raise ValueError(mode)


def project(a, w, *, mode, col_off=0, n_out=None, extra=(), rows_per_batch=None, tm=512, tn=512):
    M, K = a.shape
    n_out = w.shape[1] - col_off if n_out is None else n_out
    tm = min(tm, M)
    tn = min(tn, n_out)
    assert M % tm == 0 and n_out % tn == 0 and col_off % tn == 0
    joff = col_off // tn
    grid = (n_out // tn, M // tm)
    in_specs = [pl.BlockSpec((tm, K), lambda j, i: (i, 0)),
                pl.BlockSpec((K, tn), lambda j, i: (0, j + joff))]
    o_spec = pl.BlockSpec((tm, tn), lambda j, i: (i, j))
    f32_out = jax.ShapeDtypeStruct((M, n_out), F32)
    bf_out = jax.ShapeDtypeStruct((M, n_out), BF16)
    if mode == "plain":
        out_specs, out_shape = o_spec, f32_out
    elif mode == "headnorm":
        in_specs.append(pl.BlockSpec((1, V7X_LANES), lambda j, i: (0, 0)))
        out_specs, out_shape = (o_spec, o_spec), (f32_out, bf_out)
    elif mode == "headnorm_bf":
        in_specs.append(pl.BlockSpec((1, V7X_LANES), lambda j, i: (0, 0)))
        out_specs, out_shape = o_spec, bf_out
    elif mode == "dual":
        out_specs, out_shape = (o_spec, o_spec), (f32_out, bf_out)
    elif mode == "logsig":
        in_specs.append(pl.BlockSpec((1, tn), lambda j, i: (0, j)))
        out_specs, out_shape = o_spec, f32_out
    elif mode == "resid":
        in_specs.append(o_spec)
        if extra[1].shape[1] == 1:
            assert rows_per_batch % tm == 0
            rpt = rows_per_batch // tm
            in_specs.append(pl.BlockSpec((None, 1, tn), lambda j, i: (i // rpt, 0, j)))
        else:
            assert extra[1].shape[:2] == (1, M)
            in_specs.append(pl.BlockSpec((None, tm, tn), lambda j, i: (0, i, j)))
        out_specs, out_shape = o_spec, f32_out
    else:
        raise ValueError(mode)
    return pl.pallas_call(
        functools.partial(_proj_kernel, mode=mode),
        grid=grid,
        in_specs=in_specs,
        out_specs=out_specs,
        out_shape=out_shape,
        scratch_shapes=[pltpu.VMEM((K, tn), BF16)],
        compiler_params=_params(("parallel", "arbitrary")),
        name="project_" + mode,
    )(a, w, *extra)


def _cumsum_kernel(x_ref, o_ref, carry_ref):
    @pl.when(pl.program_id(1) == 0)
    def _():
        carry_ref[...] = jnp.zeros_like(carry_ref)

    x = x_ref[...]
    n = x.shape[0]
    tril = _iota01((n, n), lambda r, c: c <= r)
    cs = _dot01(tril, x) + carry_ref[...]
    o_ref[...] = cs
    carry_ref[...] = cs[n - 1:n, :]


def seq_cumsum(x, tc=256):
    B, L, H = x.shape
    tc = min(tc, L)
    return pl.pallas_call(
        _cumsum_kernel,
        grid=(B, L // tc),
        in_specs=[pl.BlockSpec((None, tc, H), lambda b, i: (b, i, 0))],
        out_specs=pl.BlockSpec((None, tc, H), lambda b, i: (b, i, 0)),
        out_shape=jax.ShapeDtypeStruct((B, L, H), F32),
        scratch_shapes=[pltpu.VMEM((1, H), F32)],
        compiler_params=_params(("parallel", "arbitrary")),
        name="seq_cumsum",
    )(x)


def _fox_prompt_kernel(q_ref, k_ref, v_ref, cq_ref, ck_ref, o_ref, *, tq, tk, scale):
    i = pl.program_id(2)
    h = pl.program_id(1)
    q = q_ref[...]
    cq_all = cq_ref[...]
    lane = lax.broadcasted_iota(I32, cq_all.shape, 1)
    cq = jnp.sum(jnp.where(lane == h, cq_all, 0.0), axis=-1, keepdims=True)
    hd = q.shape[-1]

    def step(j, carry, masked):
        m, l, acc = carry
        ks = pl.multiple_of(j * tk, tk)
        kb = k_ref[pl.ds(ks, tk), :]
        vb = v_ref[pl.ds(ks, tk), :]
        s = lax.dot_general(q, kb, (((1,), (1,)), ((), ())), preferred_element_type=F32) * scale
        s = s + (cq - ck_ref[:, pl.ds(ks, tk)])
        if masked:
            qpos = i * tq + lax.broadcasted_iota(I32, (tq, tk), 0)
            kpos = j * tk + lax.broadcasted_iota(I32, (tq, tk), 1)
            s = jnp.where(kpos <= qpos, s, NEG_INF)
        m_new = jnp.maximum(m, jnp.max(s, axis=-1, keepdims=True))
        a = jnp.exp(m - m_new)
        p = jnp.exp(s - m_new)
        l = l * a + jnp.sum(p, axis=-1, keepdims=True)
        acc = acc * a + jnp.dot(p.astype(BF16), vb, preferred_element_type=F32)
        return m_new, l, acc

    init = (jnp.full((tq, 1), NEG_INF, F32), jnp.zeros((tq, 1), F32), jnp.zeros((tq, hd), F32))
    nfull = (i * tq) // tk
    carry = lax.fori_loop(0, nfull, lambda j, c: step(j, c, False), init)
    for d in range(tq // tk):
        carry = step(nfull + d, carry, True)
    m, l, acc = carry
    o_ref[...] = (acc / l).astype(o_ref.dtype)


def fox_prompt_attention(q, k, v, cum, cum_t, n_heads, tq=512, tk=512):
    B, L, D = q.shape
    hd = D // n_heads
    tq = min(tq, L)
    tk = min(tk, tq)
    assert L % tq == 0 and tq % tk == 0
    kern = functools.partial(_fox_prompt_kernel, tq=tq, tk=tk, scale=hd ** -0.5)
    return pl.pallas_call(
        kern,
        grid=(B, n_heads, L // tq),
        in_specs=[pl.BlockSpec((None, tq, hd), lambda b, h, i: (b, i, h)),
                  pl.BlockSpec((None, L, hd), lambda b, h, i: (b, 0, h)),
                  pl.BlockSpec((None, L, hd), lambda b, h, i: (b, 0, h)),
                  pl.BlockSpec((None, tq, n_heads), lambda b, h, i: (b, i, 0)),
                  pl.BlockSpec((None, None, 1, L), lambda b, h, i: (b, h, 0, 0))],
        out_specs=pl.BlockSpec((None, tq, hd), lambda b, h, i: (b, i, h)),
        out_shape=jax.ShapeDtypeStruct((B, L, D), BF16),
        compiler_params=_params(("parallel", "parallel", "arbitrary")),
        name="fox_prompt_attention",
    )(q, k, v, cum, cum_t)


def _fox_tail_kernel(pt_ref, lf_ref, cn_ref, o_ref, carry_ref):
    @pl.when(pl.program_id(1) == 0)
    def _():
        carry_ref[...] = jnp.zeros_like(carry_ref)

    lf = lf_ref[...]
    n = lf.shape[0]
    upper = _iota01((n, n), lambda r, c: c > r)
    tl = _dot01(upper, lf) + carry_ref[...]
    o_ref[...] = tl + cn_ref[...]
    carry_ref[...] = tl[0:1, :] + lf[0:1, :]


def fox_sample_bias(page_table, cache_logf, cn):
    DB, NP = page_table.shape
    _, P, H = cache_logf.shape
    grid_spec = pltpu.PrefetchScalarGridSpec(
        num_scalar_prefetch=1,
        grid=(DB, NP),
        in_specs=[pl.BlockSpec((None, P, H), lambda b, p, pt: (pt[b * NP + NP - 1 - p], 0, 0)),
                  pl.BlockSpec((None, 1, H), lambda b, p, pt: (b, 0, 0))],
        out_specs=pl.BlockSpec((None, None, P, H), lambda b, p, pt: (b, NP - 1 - p, 0, 0)),
        scratch_shapes=[pltpu.VMEM((1, H), F32)],
    )
    return pl.pallas_call(
        _fox_tail_kernel,
        grid_spec=grid_spec,
        out_shape=jax.ShapeDtypeStruct((DB, NP, P, H), F32),
        compiler_params=_params(("parallel", "arbitrary")),
        name="fox_sample_bias",
    )(page_table.reshape(-1), cache_logf, cn)


def _fox_sample_kernel(pt_ref, q_ref, kn_ref, vn_ref, ck_ref, cv_ref, bias_ref, o_ref,
                       m_ref, l_ref, acc_ref, mask_ref, *, scale, n_heads):
    p = pl.program_id(1)

    @pl.when(p == 0)
    def _():
        m_ref[...] = jnp.full_like(m_ref, NEG_INF)
        l_ref[...] = jnp.zeros_like(l_ref)
        acc_ref[...] = jnp.zeros_like(acc_ref)
        row = lax.broadcasted_iota(I32, mask_ref.shape, 0)
        col = lax.broadcasted_iota(I32, mask_ref.shape, 1)
        mask_ref[...] = jnp.where((col & (n_heads - 1)) == row, 0.0, NEG_INF)

    q = q_ref[...]
    kb = ck_ref[...].astype(BF16)
    s = lax.dot_general(q.astype(BF16), kb, (((1,), (1,)), ((), ())), preferred_element_type=F32) * scale
    s = s + bias_ref[...] + mask_ref[...]
    m = m_ref[...]
    m_new = jnp.maximum(m, jnp.max(s, axis=-1, keepdims=True))
    a = jnp.exp(m - m_new)
    pe = jnp.exp(s - m_new)
    l = l_ref[...] * a + jnp.sum(pe, axis=-1, keepdims=True)
    acc = acc_ref[...] * a + jnp.dot(pe.astype(BF16), cv_ref[...].astype(BF16), preferred_element_type=F32)
    m_ref[...] = m_new
    l_ref[...] = l
    acc_ref[...] = acc

    @pl.when(p == pl.num_programs(1) - 1)
    def _():
        s_self = jnp.sum(q * kn_ref[...], axis=-1, keepdims=True) * scale
        m_fin = jnp.maximum(m_new, s_self)
        a_fin = jnp.exp(m_new - m_fin)
        p_self = jnp.exp(s_self - m_fin)
        l_fin = l * a_fin + p_self
        acc_fin = acc * a_fin + p_self * vn_ref[...]
        o_ref[...] = (acc_fin / l_fin).astype(o_ref.dtype)


def fox_sample_attention(page_table, q, k_new, v_new, cache_k, cache_v, bias):
    DB, NP = page_table.shape
    _, H, hd = q.shape
    PH = cache_k.shape[1]
    assert H & (H - 1) == 0
    tok = pl.BlockSpec((None, H, hd), lambda b, p, pt: (b, 0, 0))
    page = pl.BlockSpec((None, PH, hd), lambda b, p, pt: (pt[b * NP + p], 0, 0))
    grid_spec = pltpu.PrefetchScalarGridSpec(
        num_scalar_prefetch=1,
        grid=(DB, NP),
        in_specs=[tok, tok, tok, page, page,
                  pl.BlockSpec((None, None, 1, PH), lambda b, p, pt: (b, p, 0, 0))],
        out_specs=tok,
        scratch_shapes=[pltpu.VMEM((H, 1), F32), pltpu.VMEM((H, 1), F32), pltpu.VMEM((H, hd), F32),
                        pltpu.VMEM((H, PH), F32)],
    )
    return pl.pallas_call(
        functools.partial(_fox_sample_kernel, scale=hd ** -0.5, n_heads=H),
        grid_spec=grid_spec,
        out_shape=jax.ShapeDtypeStruct((DB, H, hd), BF16),
        compiler_params=_params(("parallel", "arbitrary")),
        name="fox_sample_attention",
    )(page_table.reshape(-1), q, k_new, v_new, cache_k, cache_v, bias)


def _hgrn_gates(zq, zf, lb):
    qt = _silu(zq)
    g = _logaddexp(jnp.log(lb), jnp.log1p(-lb) + _log_sigmoid(zf))
    kk = (1.0 - lb) * _sigmoid(-zf)
    return qt, kk, g


def _hgrn_prompt_kernel(zq_ref, zf_ref, zi_ref, zg_ref, lb_ref, on_ref, s0_ref, o_ref, sout_ref,
                        st_ref, oi_ref, *, tb):
    t = pl.program_id(2)
    C = REC_SUB
    nc = tb // C

    @pl.when(t == 0)
    def _():
        st_ref[...] = s0_ref[...].T

    qt, kk, g = _hgrn_gates(zq_ref[...], zf_ref[...], lb_ref[...])
    v = zi_ref[...]
    kdim = qt.shape[-1]
    sh = C.bit_length() - 1
    same = lambda r, c: (r >> sh) == (c >> sh)
    b = _dot01(_iota01((tb, tb), lambda r, c: same(r, c) & (c <= r)), g)
    btot = _dot01(_iota01((tb, tb), same), g)
    qe = (qt * jnp.exp(b)).astype(BF16)
    ke = (kk * jnp.exp(btot - b)).astype(BF16)
    dl = jnp.exp(btot)
    vt = v.T.astype(BF16)

    for c in range(nc):
        rows = slice(c * C, (c + 1) * C)
        st = st_ref[...]
        oi_ref[rows, :] = lax.dot_general(qe[rows, :], st.astype(BF16), (((1,), (1,)), ((), ())),
                                          preferred_element_type=F32)
        kv = jnp.dot(vt[:, rows], ke[rows, :], preferred_element_type=F32)
        st_ref[...] = st * dl[c * C:c * C + 1, :] + kv

    q3 = qt.reshape(nc, C, kdim)
    k3 = kk.reshape(nc, C, kdim)
    b3 = b.reshape(nc, C, kdim)
    v3 = v.reshape(nc, C, kdim)
    half = V7X_SUBLANES
    outs = []
    for hh in range(C // half):
        qh = q3[:, hh * half:(hh + 1) * half, :]
        bh = b3[:, hh * half:(hh + 1) * half, :]
        tau = hh * half + lax.broadcasted_iota(I32, qh.shape, 1)
        acc = jnp.zeros(qh.shape, F32)
        for sig in range(min(C, (hh + 1) * half)):
            w = jnp.exp(jnp.where(tau >= sig, bh - b3[:, sig:sig + 1, :], NEG_INF))
            a = jnp.sum(qh * k3[:, sig:sig + 1, :] * w, axis=-1, keepdims=True)
            acc = acc + a * v3[:, sig:sig + 1, :]
        outs.append(acc)
    o_intra = jnp.concatenate(outs, axis=1).reshape(tb, kdim)

    o = oi_ref[...] + o_intra
    y = o * lax.rsqrt(jnp.mean(o * o, axis=-1, keepdims=True) + EPS) * on_ref[...]
    o_ref[...] = (y * _silu(zg_ref[...])).astype(o_ref.dtype)

    @pl.when(t == pl.num_programs(2) - 1)
    def _():
        sout_ref[...] = st_ref[...].T


def hgrn_prompt(z, lb, o_norm, s0, n_heads, tb=256):
    B, L, D4 = z.shape
    D = D4 // 4
    hd = D // n_heads
    tb = min(tb, L)
    assert L % tb == 0 and tb % REC_SUB == 0
    zspec = lambda part: pl.BlockSpec((None, tb, hd), lambda b, h, t: (b, t, part * n_heads + h))
    sspec = pl.BlockSpec((None, None, hd, hd), lambda b, h, t: (b, h, 0, 0))
    return pl.pallas_call(
        functools.partial(_hgrn_prompt_kernel, tb=tb),
        grid=(B, n_heads, L // tb),
        in_specs=[zspec(0), zspec(1), zspec(2), zspec(3),
                  pl.BlockSpec((1, hd), lambda b, h, t: (0, h)),
                  pl.BlockSpec((1, hd), lambda b, h, t: (0, 0)),
                  sspec],
        out_specs=(pl.BlockSpec((None, tb, hd), lambda b, h, t: (b, t, h)), sspec),
        out_shape=(jax.ShapeDtypeStruct((B, L, D), BF16), jax.ShapeDtypeStruct(s0.shape, s0.dtype)),
        scratch_shapes=[pltpu.VMEM((hd, hd), F32), pltpu.VMEM((tb, hd), F32)],
        compiler_params=_params(("parallel", "parallel", "arbitrary")),
        name="hgrn_prompt",
    )(z, z, z, z, lb, o_norm, s0)


def _hgrn_step_kernel(zc_ref, zg_ref, lbc_ref, on_ref, s0_ref, o_ref, sout_ref, *, n_seq):
    lbc = lbc_ref[...]
    rows = []
    for b in range(n_seq):
        col = lambda part: zc_ref[part, :, b:b + 1]
        qt, kk, g = _hgrn_gates(col(0), col(1), lbc)
        vrow = zg_ref[1, b:b + 1, :]
        s_new = jnp.exp(g) * s0_ref[b] + kk * vrow
        sout_ref[b] = s_new
        rows.append(jnp.sum(qt * s_new, axis=0, keepdims=True))
    o = jnp.concatenate(rows, axis=0)
    y = o * lax.rsqrt(jnp.mean(o * o, axis=-1, keepdims=True) + EPS) * on_ref[...]
    o_ref[...] = (y * _silu(zg_ref[2])).astype(o_ref.dtype)


def hgrn_step(z, lb, o_norm, s0, n_heads):
    DB, D4 = z.shape
    D = D4 // 4
    hd = D // n_heads
    z4 = z.reshape(DB, 4, D)
    zcols = jnp.transpose(z4[:, :2, :], (1, 2, 0))
    zrows = jnp.transpose(z4[:, 1:, :], (1, 0, 2))
    sspec = pl.BlockSpec((DB, None, hd, hd), lambda h: (0, h, 0, 0))
    return pl.pallas_call(
        functools.partial(_hgrn_step_kernel, n_seq=DB),
        grid=(n_heads,),
        in_specs=[pl.BlockSpec((2, hd, DB), lambda h: (0, h, 0)),
                  pl.BlockSpec((3, DB, hd), lambda h: (0, 0, h)),
                  pl.BlockSpec((hd, 1), lambda h: (h, 0)),
                  pl.BlockSpec((1, hd), lambda h: (0, 0)),
                  sspec],
        out_specs=(pl.BlockSpec((DB, hd), lambda h: (0, h)), sspec),
        out_shape=(jax.ShapeDtypeStruct((DB, D), BF16), jax.ShapeDtypeStruct(s0.shape, s0.dtype)),
        compiler_params=_params(("parallel",)),
        name="hgrn_step",
    )(zcols, zrows, lb.reshape(D, 1), o_norm, s0)


def _router_kernel(h_ref, w_ref, b_ref, ri_ref, rg_ref, cnt_ref, carry_ref):
    @pl.when(pl.program_id(0) == 0)
    def _():
        carry_ref[...] = jnp.zeros_like(carry_ref)

    hb = h_ref[...].astype(BF16)
    tm = hb.shape[0]
    lane = lax.broadcasted_iota(I32, (tm, V7X_LANES), 1)
    lane_f = lane.astype(F32)
    valid = lane < N_GROUPS
    sc = [_sigmoid(jnp.dot(hb, w_ref[j], preferred_element_type=F32)) for j in range(EXPERTS_PER_GROUP)]
    sel = [jnp.where(valid, sc[j] + b_ref[j], NEG_INF) for j in range(EXPERTS_PER_GROUP)]
    gs = None
    for i in range(EXPERTS_PER_GROUP):
        for j in range(i + 1, EXPERTS_PER_GROUP):
            pair = sel[i] + sel[j]
            gs = pair if gs is None else jnp.maximum(gs, pair)
    gmax = jnp.max(gs, axis=-1, keepdims=True)
    gidx = jnp.min(jnp.where(gs == gmax, lane_f, float(V7X_LANES)), axis=-1, keepdims=True)
    in_g = lane_f == gidx
    pick = lambda x: jnp.sum(jnp.where(in_g, x, 0.0), axis=-1, keepdims=True)
    cand = [pick(sel[j]) for j in range(EXPERTS_PER_GROUP)]
    csc = [pick(sc[j]) for j in range(EXPERTS_PER_GROUP)]

    def first_argmax(vals):
        best, idx = vals[0], jnp.zeros_like(vals[0])
        for j in range(1, len(vals)):
            upd = vals[j] > best
            best = jnp.where(upd, vals[j], best)
            idx = jnp.where(upd, float(j), idx)
        return idx

    i1 = first_argmax(cand)
    i2 = first_argmax([jnp.where(i1 == float(j), NEG_INF, cand[j]) for j in range(EXPERTS_PER_GROUP)])
    at = lambda idx: sum(jnp.where(idx == float(j), csc[j], 0.0) for j in range(EXPERTS_PER_GROUP))
    w1, w2 = at(i1), at(i2)
    wsum = w1 + w2
    e1 = gidx * float(EXPERTS_PER_GROUP) + i1
    e2 = gidx * float(EXPERTS_PER_GROUP) + i2

    hit1 = lane_f == e1
    hit2 = lane_f == e2
    onehot = jnp.where(hit1 | hit2, 1.0, 0.0)
    lower = _iota01((tm, tm), lambda r, c: c < r)
    prior = jnp.dot(lower, onehot.astype(BF16), preferred_element_type=F32) + carry_ref[...]
    r1 = jnp.sum(jnp.where(hit1, prior, 0.0), axis=-1, keepdims=True)
    r2 = jnp.sum(jnp.where(hit2, prior, 0.0), axis=-1, keepdims=True)
    total = carry_ref[...] + jnp.sum(onehot, axis=0, keepdims=True)
    carry_ref[...] = total
    cnt_ref[...] = total.astype(I32)

    ints = jnp.where(lane == 0, e1, jnp.where(lane == 1, e2, jnp.where(lane == 2, r1, jnp.where(lane == 3, r2, 0.0))))
    ri_ref[...] = ints.astype(I32)
    rg_ref[...] = jnp.where(lane == 0, w1 / wsum, jnp.where(lane == 1, w2 / wsum, 0.0))


def moe_route(h, router_w, router_b, tm=256):
    T, D = h.shape
    tm = min(tm, T)
    pad = V7X_LANES - N_GROUPS
    wg = router_w.reshape(D, N_GROUPS, EXPERTS_PER_GROUP).transpose(2, 0, 1)
    wg = jnp.pad(wg, ((0, 0), (0, 0), (0, pad))).astype(BF16)
    bg = jnp.pad(router_b.astype(F32).reshape(N_GROUPS, EXPERTS_PER_GROUP).T, ((0, 0), (0, pad)))
    bg = bg.reshape(EXPERTS_PER_GROUP, 1, V7X_LANES)
    row = pl.BlockSpec((tm, V7X_LANES), lambda i: (i, 0))
    ri, rg, cnt = pl.pallas_call(
        _router_kernel,
        grid=(T // tm,),
        in_specs=[pl.BlockSpec((tm, D), lambda i: (i, 0)),
                  pl.BlockSpec((EXPERTS_PER_GROUP, D, V7X_LANES), lambda i: (0, 0, 0)),
                  pl.BlockSpec((EXPERTS_PER_GROUP, 1, V7X_LANES), lambda i: (0, 0, 0))],
        out_specs=(row, row, pl.BlockSpec((1, V7X_LANES), lambda i: (0, 0))),
        out_shape=(jax.ShapeDtypeStruct((T, V7X_LANES), I32), jax.ShapeDtypeStruct((T, V7X_LANES), F32),
                   jax.ShapeDtypeStruct((1, V7X_LANES), I32)),
        scratch_shapes=[pltpu.VMEM((1, V7X_LANES), F32)],
        compiler_params=_params(("arbitrary",)),
        name="moe_route",
    )(h, wg, bg)
    return ri[:, 0:2], ri[:, 2:4], rg, cnt[0, :N_EXPERTS]


def _dispatch_kernel(dest_ref, h_ref, init_ref, xb_ref, sem, *, tm):
    del init_ref
    base = pl.program_id(0) * tm

    def row_copy(r, k):
        d = dest_ref[(base + r) * TOP_K + k]
        return pltpu.make_async_copy(h_ref.at[pl.ds(r, 1)], xb_ref.at[pl.ds(d, 1)], sem)

    def start(r, _):
        for k in range(TOP_K):
            row_copy(r, k).start()
        return 0

    def wait(r, _):
        for k in range(TOP_K):
            row_copy(r, k).wait()
        return 0

    lax.fori_loop(0, tm, start, 0)
    lax.fori_loop(0, tm, wait, 0)


def moe_dispatch(h, dest, cap, tm=256):
    T, D = h.shape
    tm = min(tm, T)
    grid_spec = pltpu.PrefetchScalarGridSpec(
        num_scalar_prefetch=1,
        grid=(T // tm,),
        in_specs=[pl.BlockSpec((tm, D), lambda i, d: (i, 0)),
                  pl.BlockSpec(memory_space=pl.ANY)],
        out_specs=pl.BlockSpec(memory_space=pl.ANY),
        scratch_shapes=[pltpu.SemaphoreType.DMA(())],
    )
    return pl.pallas_call(
        functools.partial(_dispatch_kernel, tm=tm),
        grid_spec=grid_spec,
        out_shape=jax.ShapeDtypeStruct((cap, D), F32),
        input_output_aliases={2: 0},
        compiler_params=_params(("arbitrary",)),
        name="moe_dispatch",
    )(dest, h, jnp.zeros((cap, D), F32))


def _expert_kernel(be_ref, bs_ref, nu_ref, x_ref, wg_ref, wu_ref, wd_ref, y_ref, xbf_ref):
    i = pl.program_id(0)
    c = pl.program_id(1)

    @pl.when(i < nu_ref[0])
    def _():
        @pl.when(c == 0)
        def _():
            xbf_ref[...] = x_ref[...].astype(BF16)

        xb = xbf_ref[...]
        g = jnp.dot(xb, wg_ref[...].astype(BF16), preferred_element_type=F32)
        u = jnp.dot(xb, wu_ref[...].astype(BF16), preferred_element_type=F32)
        a = (_silu(g) * u).astype(BF16)
        y = jnp.dot(a, wd_ref[...].astype(BF16), preferred_element_type=F32)

        @pl.when(c == 0)
        def _():
            y_ref[...] = y

        @pl.when(c > 0)
        def _():
            y_ref[...] += y

    @pl.when((i >= nu_ref[0]) & (c == 0))
    def _():
        y_ref[...] = jnp.zeros_like(y_ref)


def moe_experts(xb, block_e, block_src, n_used, w_gate, w_up, w_down, blk, ce=256):
    cap, D = xb.shape
    E, _, DE = w_gate.shape
    nblk = cap // blk
    ce = min(ce, DE)
    nch = DE // ce
    cmap = lambda i, c, nu: jnp.where(i < nu[0], c, nch - 1)
    grid_spec = pltpu.PrefetchScalarGridSpec(
        num_scalar_prefetch=3,
        grid=(nblk, nch),
        in_specs=[pl.BlockSpec((blk, D), lambda i, c, be, bs, nu: (bs[i], 0)),
                  pl.BlockSpec((None, D, ce), lambda i, c, be, bs, nu: (be[i], 0, cmap(i, c, nu))),
                  pl.BlockSpec((None, D, ce), lambda i, c, be, bs, nu: (be[i], 0, cmap(i, c, nu))),
                  pl.BlockSpec((None, ce, D), lambda i, c, be, bs, nu: (be[i], cmap(i, c, nu), 0))],
        out_specs=pl.BlockSpec((blk, D), lambda i, c, be, bs, nu: (i, 0)),
        scratch_shapes=[pltpu.VMEM((blk, D), BF16)],
    )
    return pl.pallas_call(
        _expert_kernel,
        grid_spec=grid_spec,
        out_shape=jax.ShapeDtypeStruct((cap, D), F32),
        compiler_params=_params(("arbitrary", "arbitrary")),
        name="moe_experts",
    )(block_e, block_src, n_used, xb, w_gate, w_up, w_down)


def _combine_kernel(dest_ref, yb_ref, x_ref, gate_ref, rg_ref, o_ref, ybuf_ref, sem, *, tm):
    base = pl.program_id(0) * tm

    def row_copy(r, k):
        d = dest_ref[(base + r) * TOP_K + k]
        return pltpu.make_async_copy(yb_ref.at[pl.ds(d, 1)], ybuf_ref.at[k, pl.ds(r, 1)], sem)

    def start(r, _):
        for k in range(TOP_K):
            row_copy(r, k).start()
        return 0

    def wait(r, _):
        for k in range(TOP_K):
            row_copy(r, k).wait()
        return 0

    lax.fori_loop(0, tm, start, 0)
    lax.fori_loop(0, tm, wait, 0)
    rg = rg_ref[...]
    y = ybuf_ref[0] * rg[:, 0:1] + ybuf_ref[1] * rg[:, 1:2]
    o_ref[...] = x_ref[...] + gate_ref[...] * y


def moe_combine(yb, dest, x, gate, rg, rows_per_batch, tm=256):
    T, D = x.shape
    tm = min(tm, T)
    if gate.shape[1] == 1:
        rpt = rows_per_batch // tm
        gspec = pl.BlockSpec((None, 1, D), lambda i, d: (i // rpt, 0, 0))
    else:
        gspec = pl.BlockSpec((None, tm, D), lambda i, d: (0, i, 0))
    grid_spec = pltpu.PrefetchScalarGridSpec(
        num_scalar_prefetch=1,
        grid=(T // tm,),
        in_specs=[pl.BlockSpec(memory_space=pl.ANY),
                  pl.BlockSpec((tm, D), lambda i, d: (i, 0)),
                  gspec,
                  pl.BlockSpec((tm, V7X_LANES), lambda i, d: (i, 0))],
        out_specs=pl.BlockSpec((tm, D), lambda i, d: (i, 0)),
        scratch_shapes=[pltpu.VMEM((TOP_K, tm, D), F32), pltpu.SemaphoreType.DMA(())],
    )
    return pl.pallas_call(
        functools.partial(_combine_kernel, tm=tm),
        grid_spec=grid_spec,
        out_shape=jax.ShapeDtypeStruct((T, D), F32),
        compiler_params=_params(("arbitrary",)),
        name="moe_combine",
    )(dest, yb, x, gate, rg)


def moe_sublayer(h, x, gate, rows_per_batch, router_w, router_b, w_gate, w_up, w_down, blk):
    T, D = h.shape
    e_idx, rank, rg, counts = moe_route(h, router_w, router_b, tm=min(256, T))
    A = T * TOP_K
    cap = -(-(A + N_EXPERTS * (blk - 1)) // blk) * blk
    nblk = cap // blk
    padded = (counts + blk - 1) // blk * blk
    pad_end = jnp.cumsum(padded)
    pad_start = pad_end - padded
    dest = (pad_start[e_idx] + rank).reshape(-1).astype(I32)
    n_used = (pad_end[-1] // blk).astype(I32)
    block_src = jnp.minimum(jnp.arange(nblk, dtype=I32), n_used - 1)
    block_e = jnp.minimum(jnp.searchsorted(pad_end, block_src * blk, side="right"), N_EXPERTS - 1).astype(I32)
    xb = moe_dispatch(h, dest, cap, tm=min(256, T))
    yb = moe_experts(xb, block_e, block_src, n_used.reshape(1), w_gate, w_up, w_down, blk)
    return moe_combine(yb, dest, x, gate, rg, rows_per_batch, tm=min(256, T))


def kernel(x_prompt, x_sample, c_prompt, c_sample, cache_k, cache_v, cache_logf, state_hgrn, page_table,
           ada_w, ada_b, norm_g, att_w_in, att_b_f, att_q_norm, att_k_norm, att_w_o,
           rec_w_in, rec_lb_logits, rec_o_norm, rec_w_o, router_w, router_b,
           moe_w_gate, moe_w_up, moe_w_down):
    B, L, D = x_prompt.shape
    DB, LS, _ = x_sample.shape
    depth = ada_w.shape[0]
    H = ATT_HEADS
    hd = D // H
    T = B * L
    TS = DB * LS
    assert LS == 1
    P = cache_k.shape[2]

    p_lb = jax.nn.softmax(rec_lb_logits.astype(F32), axis=0)
    lb_table = jnp.cumsum(p_lb, axis=0) - p_lb[0]

    n_c = B + DB
    r_c = -(-n_c // V7X_SUBLANES) * V7X_SUBLANES
    c_all = jnp.pad(jnp.concatenate([c_prompt, c_sample], axis=0), ((0, r_c - n_c), (0, 0)))
    mod = ada_modulation(c_all, ada_w.reshape(depth * 2, D, 3 * D), ada_b.reshape(depth * 2, 1, 3 * D))
    mod = mod.reshape(depth, 2, r_c, 3, D)

    def mods(layer, sub):
        m = mod[layer, sub]
        shift, scale, gate = m[:, 0], m[:, 1], m[:, 2]
        pr = lambda a: a[:B].reshape(B, 1, D)
        sa = lambda a: a[B:n_c].reshape(1, DB, D)
        return (pr(shift), pr(scale), pr(gate)), (sa(shift), sa(scale), sa(gate))

    xp = x_prompt.reshape(T, D)
    xs = x_sample.reshape(TS, D)
    outs = {n: [] for n in ("kp", "vp", "lfp", "ks", "vs", "lfs", "sp", "ss")}
    blk_p = max(8, min(256, T * TOP_K // N_EXPERTS))
    blk_s = max(8, min(256, TS * TOP_K // N_EXPERTS))

    for layer in range(depth):
        j = layer // 2
        (shp, scp, gp), (shs, scs, gs) = mods(layer, 0)
        g0 = norm_g[layer, 0].reshape(1, D)
        hp = norm_modulate(xp.reshape(B, L, D), g0, scp, shp).reshape(T, D)
        hs = norm_modulate(xs.reshape(1, TS, D), g0, scs, shs).reshape(TS, D)
        if layer % 2 == 0:
            w_in = att_w_in[j]
            w_f = w_in[:, 3 * D:]
            b_f = att_b_f[j].reshape(1, H)
            qg = att_q_norm[j].reshape(1, hd)
            kg = att_k_norm[j].reshape(1, hd)
            q_bf = project(hp, w_in, mode="headnorm_bf", col_off=0, n_out=D, extra=(qg,))
            k_f, k_bf = project(hp, w_in, mode="headnorm", col_off=D, n_out=D, extra=(kg,))
            v_f, v_bf = project(hp, w_in, mode="dual", col_off=2 * D, n_out=D)
            lf = project(hp, w_f, mode="logsig", extra=(b_f,)).reshape(B, L, H)
            cum = seq_cumsum(lf)
            cum_t = jnp.transpose(cum, (0, 2, 1)).reshape(B, H, 1, L)
            op = fox_prompt_attention(q_bf.reshape(B, L, D), k_bf.reshape(B, L, D), v_bf.reshape(B, L, D),
                                      cum, cum_t, H)
            xp = project(op.reshape(T, D), att_w_o[j], mode="resid", extra=(xp, gp), rows_per_batch=L)
            outs["kp"].append(k_f.reshape(B, L, H, hd))
            outs["vp"].append(v_f.reshape(B, L, H, hd))
            outs["lfp"].append(lf)
            qs, _ = project(hs, w_in, mode="headnorm", col_off=0, n_out=D, extra=(qg,))
            ks, _ = project(hs, w_in, mode="headnorm", col_off=D, n_out=D, extra=(kg,))
            vs = project(hs, w_in, mode="plain", col_off=2 * D, n_out=D)
            lfs = project(hs, w_f, mode="logsig", extra=(b_f,))
            bias = fox_sample_bias(page_table, cache_logf[j], lfs.reshape(DB, 1, H))
            n_pool = cache_k.shape[1]
            osm = fox_sample_attention(page_table, qs.reshape(DB, H, hd), ks.reshape(DB, H, hd),
                                       vs.reshape(DB, H, hd), cache_k[j].reshape(n_pool, P * H, hd),
                                       cache_v[j].reshape(n_pool, P * H, hd),
                                       bias.reshape(DB, bias.shape[1], 1, P * H))
            xs = project(osm.reshape(TS, D), att_w_o[j], mode="resid", extra=(xs, gs))
            outs["ks"].append(ks.reshape(DB, LS, H, hd))
            outs["vs"].append(vs.reshape(DB, LS, H, hd))
            outs["lfs"].append(lfs.reshape(DB, LS, H))
        else:
            lb = lb_table[layer].reshape(1, D)
            on = rec_o_norm[j].reshape(1, hd)
            zp = project(hp, rec_w_in[j], mode="plain")
            s0p = jnp.zeros((B, REC_HEADS, hd, hd), state_hgrn.dtype)
            op, sp = hgrn_prompt(zp.reshape(B, L, 4 * D), lb, on, s0p, REC_HEADS)
            xp = project(op.reshape(T, D), rec_w_o[j], mode="resid", extra=(xp, gp), rows_per_batch=L)
            zs = project(hs, rec_w_in[j], mode="plain")
            osm, ss = hgrn_step(zs, lb_table[layer], on, state_hgrn[j], REC_HEADS)
            xs = project(osm, rec_w_o[j], mode="resid", extra=(xs, gs))
            outs["sp"].append(sp)
            outs["ss"].append(ss)

        (shp, scp, gp), (shs, scs, gs) = mods(layer, 1)
        g1 = norm_g[layer, 1].reshape(1, D)
        hp = norm_modulate(xp.reshape(B, L, D), g1, scp, shp, out_dtype=F32).reshape(T, D)
        hs = norm_modulate(xs.reshape(1, TS, D), g1, scs, shs, out_dtype=F32).reshape(TS, D)
        moe_w = (router_w, router_b, moe_w_gate[layer], moe_w_up[layer], moe_w_down[layer])
        xp = moe_sublayer(hp, xp, gp, L, *moe_w, blk=blk_p)
        xs = moe_sublayer(hs, xs, gs, None, *moe_w, blk=blk_s)

    st = lambda n: jnp.stack(outs[n])
    return (xp.reshape(B, L, D), xs.reshape(DB, LS, D), st("kp"), st("vp"), st("lfp"), st("ks"), st("vs"),
            st("lfs"), st("sp"), st("ss"))
```

```python
import functools

import jax
import jax.numpy as jnp
from jax import lax
from jax.experimental import pallas as pl
from jax.experimental.pallas import tpu as pltpu

F32 = jnp.float32
BF16 = jnp.bfloat16
I32 = jnp.int32
EPS = 1e-6

V7X_LANES = 128
V7X_SUBLANES = 8
V7X_VMEM_LIMIT_BYTES = 56 * 1024 * 1024

ATT_HEADS = 32
REC_HEADS = 32
N_GROUPS = 8
EXPERTS_PER_GROUP = 4
N_EXPERTS = N_GROUPS * EXPERTS_PER_GROUP
TOP_K = 2
REC_SUB = 16
NEG_INF = float("-inf")
LOG2E = 1.4426950408889634


def _params(sem):
    return pltpu.CompilerParams(dimension_semantics=sem, vmem_limit_bytes=V7X_VMEM_LIMIT_BYTES)


def _split3(x):
    x1 = x.astype(BF16)
    r1 = x - x1.astype(F32)
    x2 = r1.astype(BF16)
    x3 = (r1 - x2.astype(F32)).astype(BF16)
    return x1, x2, x3


def _dot01(m01, x):
    x1, x2, x3 = _split3(x)
    d = lambda b: jnp.dot(m01, b, preferred_element_type=F32)
    return d(x1) + (d(x2) + d(x3))


def _iota01(shape, pred):
    r = lax.broadcasted_iota(I32, shape, 0)
    c = lax.broadcasted_iota(I32, shape, 1)
    return jnp.where(pred(r, c), 1.0, 0.0).astype(BF16)


def _sigmoid(x):
    return 1.0 / (1.0 + jnp.exp(-x))


def _silu(x):
    return x * _sigmoid(x)


def _log_sigmoid(x):
    return jnp.minimum(x, 0.0) - jnp.log1p(jnp.exp(-jnp.abs(x)))


def _logaddexp(a, b):
    return jnp.maximum(a, b) + jnp.log1p(jnp.exp(-jnp.abs(a - b)))


def _ada_kernel(c_ref, w_ref, b_ref, o_ref):
    sc = _silu(c_ref[...]).astype(BF16)
    o_ref[...] = jnp.dot(sc, w_ref[...].astype(BF16), preferred_element_type=F32) + b_ref[...]


def ada_modulation(c_all, ada_w, ada_b, tn=512):
    S, D, N = ada_w.shape
    R = c_all.shape[0]
    tn = min(tn, N)
    return pl.pallas_call(
        _ada_kernel,
        grid=(S, N // tn),
        in_specs=[pl.BlockSpec((R, D), lambda s, j: (0, 0)),
                  pl.BlockSpec((None, D, tn), lambda s, j: (s, 0, j)),
                  pl.BlockSpec((None, 1, tn), lambda s, j: (s, 0, j))],
        out_specs=pl.BlockSpec((None, R, tn), lambda s, j: (s, 0, j)),
        out_shape=jax.ShapeDtypeStruct((S, R, N), F32),
        compiler_params=_params(("parallel", "parallel")),
        name="ada_modulation",
    )(c_all, ada_w, ada_b)


def _norm_mod_kernel(x_ref, g_ref, scale_ref, shift_ref, o_ref):
    x = x_ref[...]
    y = x * lax.rsqrt(jnp.mean(x * x, axis=-1, keepdims=True) + EPS)
    h = (y * g_ref[...]) * (1.0 + scale_ref[...]) + shift_ref[...]
    o_ref[...] = h.astype(o_ref.dtype)


def norm_modulate(x, g, scale, shift, out_dtype=BF16, tr=256):
    NB, L, D = x.shape
    LS = scale.shape[1]
    tr = min(tr, L)
    ts = tr if LS == L else 1
    smap = (lambda b, i: (b, i, 0)) if LS == L else (lambda b, i: (b, 0, 0))
    return pl.pallas_call(
        _norm_mod_kernel,
        grid=(NB, L // tr),
        in_specs=[pl.BlockSpec((None, tr, D), lambda b, i: (b, i, 0)),
                  pl.BlockSpec((1, D), lambda b, i: (0, 0)),
                  pl.BlockSpec((None, ts, D), smap),
                  pl.BlockSpec((None, ts, D), smap)],
        out_specs=pl.BlockSpec((None, tr, D), lambda b, i: (b, i, 0)),
        out_shape=jax.ShapeDtypeStruct((NB, L, D), out_dtype),
        compiler_params=_params(("parallel", "parallel")),
        name="norm_modulate",
    )(x, g, scale, shift)


def _head_rms(acc, g):
    parts = []
    for c in range(acc.shape[1] // V7X_LANES):
        blk = acc[:, c * V7X_LANES:(c + 1) * V7X_LANES]
        y = blk * lax.rsqrt(jnp.mean(blk * blk, axis=-1, keepdims=True) + EPS)
        parts.append(y * g)
    return jnp.concatenate(parts, axis=-1) if len(parts) > 1 else parts[0]


def _proj_kernel(*refs, mode):
    a_ref, w_ref = refs[0], refs[1]
    wbf_ref = refs[-1]

    @pl.when(pl.program_id(1) == 0)
    def _():
        wbf_ref[...] = w_ref[...].astype(BF16)

    acc = jnp.dot(a_ref[...], wbf_ref[...], preferred_element_type=F32)
    if mode == "plain":
        refs[2][...] = acc
    elif mode == "headnorm":
        y = _head_rms(acc, refs[2][...])
        refs[3][...] = y
        refs[4][...] = y.astype(BF16)
    elif mode == "headnorm_bf":
        refs[3][...] = _head_rms(acc, refs[2][...]).astype(BF16)
    elif mode == "dual":
        refs[2][...] = acc
        refs[3][...] = acc.astype(BF16)
    elif mode == "logsig":
        refs[3][...] = _log_sigmoid(acc + refs[2][...])
    elif mode == "resid":
        refs[4][...] = refs[2][...] + refs[3][...] * acc
    else:
        raise ValueError(mode)


def project(a, w, *, mode, layer=0, col_off=0, n_out=None, extra=(), rows_per_batch=None, tm=1024, tn=512):
    M, K = a.shape
    n_out = w.shape[2] - col_off if n_out is None else n_out
    tm = min(tm, M)
    if mode == "resid" and extra[1].shape[1] == 1:
        tm = min(tm, rows_per_batch)
    tn = min(tn, n_out)
    assert M % tm == 0 and n_out % tn == 0 and col_off % tn == 0
    joff = col_off // tn
    grid = (n_out // tn, M // tm)
    in_specs = [pl.BlockSpec((tm, K), lambda j, i: (i, 0)),
                pl.BlockSpec((None, K, tn), lambda j, i: (layer, 0, j + joff))]
    o_spec = pl.BlockSpec((tm, tn), lambda j, i: (i, j))
    f32_out = jax.ShapeDtypeStruct((M, n_out), F32)
    bf_out = jax.ShapeDtypeStruct((M, n_out), BF16)
    if mode == "plain":
        out_specs, out_shape = o_spec, f32_out
    elif mode == "headnorm":
        in_specs.append(pl.BlockSpec((1, V7X_LANES), lambda j, i: (0, 0)))
        out_specs, out_shape = (o_spec, o_spec), (f32_out, bf_out)
    elif mode == "headnorm_bf":
        in_specs.append(pl.BlockSpec((1, V7X_LANES), lambda j, i: (0, 0)))
        out_specs, out_shape = o_spec, bf_out
    elif mode == "dual":
        out_specs, out_shape = (o_spec, o_spec), (f32_out, bf_out)
    elif mode == "logsig":
        in_specs.append(pl.BlockSpec((1, tn), lambda j, i: (0, j)))
        out_specs, out_shape = o_spec, f32_out
    elif mode == "resid":
        in_specs.append(o_spec)
        if extra[1].shape[1] == 1:
            assert rows_per_batch % tm == 0
            rpt = rows_per_batch // tm
            in_specs.append(pl.BlockSpec((None, 1, tn), lambda j, i: (i // rpt, 0, j)))
        else:
            assert extra[1].shape[:2] == (1, M)
            in_specs.append(pl.BlockSpec((None, tm, tn), lambda j, i: (0, i, j)))
        out_specs, out_shape = o_spec, f32_out
    else:
        raise ValueError(mode)
    return pl.pallas_call(
        functools.partial(_proj_kernel, mode=mode),
        grid=grid,
        in_specs=in_specs,
        out_specs=out_specs,
        out_shape=out_shape,
        scratch_shapes=[pltpu.VMEM((K, tn), BF16)],
        compiler_params=_params(("parallel", "arbitrary")),
        name="project_" + mode,
    )(a, w, *extra)


def _cumsum_kernel(x_ref, o_ref, carry_ref):
    @pl.when(pl.program_id(1) == 0)
    def _():
        carry_ref[...] = jnp.zeros_like(carry_ref)

    x = x_ref[...]
    n = x.shape[0]
    tril = _iota01((n, n), lambda r, c: c <= r)
    cs = _dot01(tril, x) + carry_ref[...]
    o_ref[...] = cs
    carry_ref[...] = cs[n - 1:n, :]


def seq_cumsum(x, tc=256):
    B, L, H = x.shape
    tc = min(tc, L)
    return pl.pallas_call(
        _cumsum_kernel,
        grid=(B, L // tc),
        in_specs=[pl.BlockSpec((None, tc, H), lambda b, i: (b, i, 0))],
        out_specs=pl.BlockSpec((None, tc, H), lambda b, i: (b, i, 0)),
        out_shape=jax.ShapeDtypeStruct((B, L, H), F32),
        scratch_shapes=[pltpu.VMEM((1, H), F32)],
        compiler_params=_params(("parallel", "arbitrary")),
        name="seq_cumsum",
    )(x)


def _fox_prompt_kernel(q_ref, k_ref, v_ref, cq_ref, ck_ref, o_ref, *, tq, tk, scale):
    i = pl.program_id(2)
    h = pl.program_id(1)
    q = q_ref[...]
    cq_all = cq_ref[...]
    lane = lax.broadcasted_iota(I32, cq_all.shape, 1)
    cq = jnp.sum(jnp.where(lane == h, cq_all, 0.0), axis=-1, keepdims=True)
    hd = q.shape[-1]

    def step(j, carry, masked):
        m, l, acc = carry
        ks = pl.multiple_of(j * tk, tk)
        kb = k_ref[pl.ds(ks, tk), :]
        vb = v_ref[pl.ds(ks, tk), :]
        s = lax.dot_general(q, kb, (((1,), (1,)), ((), ())), preferred_element_type=F32) * scale
        s = s + (cq - ck_ref[:, pl.ds(ks, tk)])
        if masked:
            qpos = i * tq + lax.broadcasted_iota(I32, (tq, tk), 0)
            kpos = j * tk + lax.broadcasted_iota(I32, (tq, tk), 1)
            s = jnp.where(kpos <= qpos, s, NEG_INF)
        m_new = jnp.maximum(m, jnp.max(s, axis=-1, keepdims=True))
        a = jnp.exp(m - m_new)
        p = jnp.exp(s - m_new)
        l = l * a + jnp.sum(p, axis=-1, keepdims=True)
        acc = acc * a + jnp.dot(p.astype(BF16), vb, preferred_element_type=F32)
        return m_new, l, acc

    init = (jnp.full((tq, 1), NEG_INF, F32), jnp.zeros((tq, 1), F32), jnp.zeros((tq, hd), F32))
    nfull = (i * tq) // tk
    carry = lax.fori_loop(0, nfull, lambda j, c: step(j, c, False), init)
    for d in range(tq // tk):
        carry = step(nfull + d, carry, True)
    m, l, acc = carry
    o_ref[...] = (acc / l).astype(o_ref.dtype)


def fox_prompt_attention(q, k, v, cum, cum_t, n_heads, tq=512, tk=512):
    B, L, D = q.shape
    hd = D // n_heads
    tq = min(tq, L)
    tk = min(tk, tq)
    assert L % tq == 0 and tq % tk == 0
    kern = functools.partial(_fox_prompt_kernel, tq=tq, tk=tk, scale=hd ** -0.5)
    return pl.pallas_call(
        kern,
        grid=(B, n_heads, L // tq),
        in_specs=[pl.BlockSpec((None, tq, hd), lambda b, h, i: (b, i, h)),
                  pl.BlockSpec((None, L, hd), lambda b, h, i: (b, 0, h)),
                  pl.BlockSpec((None, L, hd), lambda b, h, i: (b, 0, h)),
                  pl.BlockSpec((None, tq, n_heads), lambda b, h, i: (b, i, 0)),
                  pl.BlockSpec((None, None, 1, L), lambda b, h, i: (b, h, 0, 0))],
        out_specs=pl.BlockSpec((None, tq, hd), lambda b, h, i: (b, i, h)),
        out_shape=jax.ShapeDtypeStruct((B, L, D), BF16),
        compiler_params=_params(("parallel", "parallel", "arbitrary")),
        name="fox_prompt_attention",
    )(q, k, v, cum, cum_t)


def _fox_tail_kernel(pt_ref, *refs, n_seq):
    lf_refs, (cn_ref, o_ref, carry_ref) = refs[:n_seq], refs[n_seq:]

    @pl.when(pl.program_id(0) == 0)
    def _():
        carry_ref[...] = jnp.zeros_like(carry_ref)

    n = lf_refs[0].shape[0]
    upper = _iota01((n, n), lambda r, c: c > r)
    for b in range(n_seq):
        lf = lf_refs[b][...]
        tl = _dot01(upper, lf) + carry_ref[b]
        o_ref[b] = tl + cn_ref[b]
        carry_ref[b] = tl[0:1, :] + lf[0:1, :]


def fox_sample_bias(page_table, cache_logf, layer, cn):
    DB, NP = page_table.shape
    _, _, P, H = cache_logf.shape
    page = lambda b: pl.BlockSpec((None, None, P, H), lambda p, pt: (layer, pt[b * NP + NP - 1 - p], 0, 0))
    grid_spec = pltpu.PrefetchScalarGridSpec(
        num_scalar_prefetch=1,
        grid=(NP,),
        in_specs=[page(b) for b in range(DB)] + [pl.BlockSpec((DB, 1, H), lambda p, pt: (0, 0, 0))],
        out_specs=pl.BlockSpec((DB, None, P, H), lambda p, pt: (0, NP - 1 - p, 0, 0)),
        scratch_shapes=[pltpu.VMEM((DB, 1, H), F32)],
    )
    return pl.pallas_call(
        functools.partial(_fox_tail_kernel, n_seq=DB),
        grid_spec=grid_spec,
        out_shape=jax.ShapeDtypeStruct((DB, NP, P, H), F32),
        compiler_params=_params(("arbitrary",)),
        name="fox_sample_bias",
    )(page_table.reshape(-1), *([cache_logf] * DB), cn)


def _fox_sample_kernel(pt_ref, q_ref, kn_ref, vn_ref, *refs, scale, n_heads, pps):
    ck_refs, cv_refs = refs[:pps], refs[pps:2 * pps]
    bias_ref, o_ref, m_ref, l_ref, acc_ref, mask_ref = refs[2 * pps:]
    p = pl.program_id(1)

    @pl.when(p == 0)
    def _():
        m_ref[...] = jnp.full_like(m_ref, NEG_INF)
        l_ref[...] = jnp.zeros_like(l_ref)
        acc_ref[...] = jnp.zeros_like(acc_ref)
        row = lax.broadcasted_iota(I32, mask_ref.shape, 0)
        col = lax.broadcasted_iota(I32, mask_ref.shape, 1)
        mask_ref[...] = jnp.where((col & (n_heads - 1)) == row, 0.0, NEG_INF)

    q = q_ref[...]
    qb = q.astype(BF16)
    ss = []
    for i in range(pps):
        kb = ck_refs[i][...].astype(BF16)
        s = lax.dot_general(qb, kb, (((1,), (1,)), ((), ())), preferred_element_type=F32) * scale
        ss.append(s + bias_ref[i] + mask_ref[...])
    m = m_ref[...]
    m_new = m
    for s in ss:
        m_new = jnp.maximum(m_new, jnp.max(s, axis=-1, keepdims=True))
    a = jnp.exp(m - m_new)
    l = l_ref[...] * a
    acc = acc_ref[...] * a
    for i in range(pps):
        pe = jnp.exp(ss[i] - m_new)
        l = l + jnp.sum(pe, axis=-1, keepdims=True)
        acc = acc + jnp.dot(pe.astype(BF16), cv_refs[i][...].astype(BF16), preferred_element_type=F32)
    m_ref[...] = m_new
    l_ref[...] = l
    acc_ref[...] = acc

    @pl.when(p == pl.num_programs(1) - 1)
    def _():
        s_self = jnp.sum(q * kn_ref[...], axis=-1, keepdims=True) * scale
        m_fin = jnp.maximum(m_new, s_self)
        a_fin = jnp.exp(m_new - m_fin)
        p_self = jnp.exp(s_self - m_fin)
        l_fin = l * a_fin + p_self
        acc_fin = acc * a_fin + p_self * vn_ref[...]
        o_ref[...] = (acc_fin / l_fin).astype(o_ref.dtype)


def fox_sample_attention(page_table, q, k_new, v_new, cache_k, cache_v, layer, bias, pps=4):
    DB, NP = page_table.shape
    _, H, hd = q.shape
    PH = cache_k.shape[2]
    pps = min(pps, NP)
    assert H & (H - 1) == 0 and NP % pps == 0
    tok = pl.BlockSpec((None, H, hd), lambda b, p, pt: (b, 0, 0))
    page = lambda i: pl.BlockSpec((None, None, PH, hd), lambda b, p, pt: (layer, pt[b * NP + p * pps + i], 0, 0))
    pages = [page(i) for i in range(pps)]
    grid_spec = pltpu.PrefetchScalarGridSpec(
        num_scalar_prefetch=1,
        grid=(DB, NP // pps),
        in_specs=[tok, tok, tok] + pages + pages +
                 [pl.BlockSpec((None, pps, 1, PH), lambda b, p, pt: (b, p, 0, 0))],
        out_specs=tok,
        scratch_shapes=[pltpu.VMEM((H, 1), F32), pltpu.VMEM((H, 1), F32), pltpu.VMEM((H, hd), F32),
                        pltpu.VMEM((H, PH), F32)],
    )
    return pl.pallas_call(
        functools.partial(_fox_sample_kernel, scale=hd ** -0.5, n_heads=H, pps=pps),
        grid_spec=grid_spec,
        out_shape=jax.ShapeDtypeStruct((DB, H, hd), BF16),
        compiler_params=_params(("parallel", "arbitrary")),
        name="fox_sample_attention",
    )(page_table.reshape(-1), q, k_new, v_new, *([cache_k] * pps), *([cache_v] * pps), bias)


def _hgrn_gates(zq, zf, lb):
    qt = _silu(zq)
    g = _logaddexp(jnp.log(lb), jnp.log1p(-lb) + _log_sigmoid(zf))
    kk = (1.0 - lb) * _sigmoid(-zf)
    return qt, kk, g


def _hgrn_prompt_kernel(zq_ref, zf_ref, zi_ref, zg_ref, lb_ref, on_ref, s0_ref, o_ref, sout_ref,
                        st_ref, k_scr, b2_scr, kv_scr, oi_scr, *, tb):
    t = pl.program_id(2)
    C = REC_SUB
    nc = tb // C
    half = V7X_SUBLANES

    @pl.when(t == 0)
    def _():
        st_ref[...] = s0_ref[...].T

    qt, kk, g = _hgrn_gates(zq_ref[...], zf_ref[...], lb_ref[...])
    kdim = qt.shape[-1]
    sh = C.bit_length() - 1
    b = _dot01(_iota01((tb, tb), lambda r, c: ((r >> sh) == (c >> sh)) & (c <= r)), g)
    b2 = b * LOG2E
    k_scr[...] = kk
    b2_scr[...] = b2
    vt = zi_ref[...].T.astype(BF16)

    for c in range(nc):
        rows = slice(c * C, (c + 1) * C)
        btot = b[c * C + C - 1:(c + 1) * C, :]
        ke = (kk[rows, :] * jnp.exp(btot - b[rows, :])).astype(BF16)
        kv_scr[c] = jnp.dot(vt[:, rows], ke, preferred_element_type=F32)

    st = st_ref[...]
    for c in range(nc):
        rows = slice(c * C, (c + 1) * C)
        qe = (qt[rows, :] * jnp.exp(b[rows, :])).astype(BF16)
        oi_scr[rows, :] = lax.dot_general(qe, st.astype(BF16), (((1,), (1,)), ((), ())),
                                          preferred_element_type=F32)
        st = st * jnp.exp(b[c * C + C - 1:(c + 1) * C, :]) + kv_scr[c]
    st_ref[...] = st

    o_parts = []
    for c in range(nc):
        r0 = c * C
        for hh in range(C // half):
            h0 = r0 + hh * half
            qh = qt[h0:h0 + half, :]
            bh = b2[h0:h0 + half, :]
            tau = hh * half + lax.broadcasted_iota(I32, (half, kdim), 0)
            acc = jnp.zeros((half, kdim), F32)
            for sig in range(min(C, (hh + 1) * half)):
                d = bh - b2_scr[pl.ds(r0 + sig, 1), :]
                if sig > hh * half:
                    d = jnp.where(tau >= sig, d, NEG_INF)
                a = jnp.sum(qh * k_scr[pl.ds(r0 + sig, 1), :] * jnp.exp2(d), axis=-1, keepdims=True)
                acc = acc + a * zi_ref[pl.ds(r0 + sig, 1), :]
            o_parts.append(acc)

    o = jnp.concatenate(o_parts, axis=0) + oi_scr[...]
    y = o * lax.rsqrt(jnp.mean(o * o, axis=-1, keepdims=True) + EPS) * on_ref[...]
    o_ref[...] = (y * _silu(zg_ref[...])).astype(o_ref.dtype)

    @pl.when(t == pl.num_programs(2) - 1)
    def _():
        sout_ref[...] = st.T


def hgrn_prompt(z, lb, o_norm, s0, n_heads, tb=256):
    B, L, D4 = z.shape
    D = D4 // 4
    hd = D // n_heads
    tb = min(tb, L)
    assert L % tb == 0 and tb % REC_SUB == 0
    zspec = lambda part: pl.BlockSpec((None, tb, hd), lambda b, h, t: (b, t, part * n_heads + h))
    sspec = pl.BlockSpec((None, None, hd, hd), lambda b, h, t: (b, h, 0, 0))
    return pl.pallas_call(
        functools.partial(_hgrn_prompt_kernel, tb=tb),
        grid=(B, n_heads, L // tb),
        in_specs=[zspec(0), zspec(1), zspec(2), zspec(3),
                  pl.BlockSpec((1, hd), lambda b, h, t: (0, h)),
                  pl.BlockSpec((1, hd), lambda b, h, t: (0, 0)),
                  sspec],
        out_specs=(pl.BlockSpec((None, tb, hd), lambda b, h, t: (b, t, h)), sspec),
        out_shape=(jax.ShapeDtypeStruct((B, L, D), BF16), jax.ShapeDtypeStruct(s0.shape, s0.dtype)),
        scratch_shapes=[pltpu.VMEM((hd, hd), F32), pltpu.VMEM((tb, hd), F32), pltpu.VMEM((tb, hd), F32),
                        pltpu.VMEM((tb // REC_SUB, hd, hd), F32), pltpu.VMEM((tb, hd), F32)],
        compiler_params=_params(("parallel", "parallel", "arbitrary")),
        name="hgrn_prompt",
    )(z, z, z, z, lb, o_norm, s0)


def _hgrn_step_kernel(zc_ref, zg_ref, lbc_ref, on_ref, s0_ref, o_ref, sout_ref, *, n_seq):
    lbc = lbc_ref[...]
    rows = []
    for b in range(n_seq):
        col = lambda part: zc_ref[part, :, b:b + 1]
        qt, kk, g = _hgrn_gates(col(0), col(1), lbc)
        vrow = zg_ref[1, b:b + 1, :]
        s_new = jnp.exp(g) * s0_ref[b] + kk * vrow
        sout_ref[b] = s_new
        rows.append(jnp.sum(qt * s_new, axis=0, keepdims=True))
    o = jnp.concatenate(rows, axis=0)
    y = o * lax.rsqrt(jnp.mean(o * o, axis=-1, keepdims=True) + EPS) * on_ref[...]
    o_ref[...] = (y * _silu(zg_ref[2])).astype(o_ref.dtype)


def hgrn_step(z, lb, o_norm, s0, n_heads):
    DB, D4 = z.shape
    D = D4 // 4
    hd = D // n_heads
    z4 = z.reshape(DB, 4, D)
    zcols = jnp.transpose(z4[:, :2, :], (1, 2, 0))
    zrows = jnp.transpose(z4[:, 1:, :], (1, 0, 2))
    sspec = pl.BlockSpec((DB, None, hd, hd), lambda h: (0, h, 0, 0))
    return pl.pallas_call(
        functools.partial(_hgrn_step_kernel, n_seq=DB),
        grid=(n_heads,),
        in_specs=[pl.BlockSpec((2, hd, DB), lambda h: (0, h, 0)),
                  pl.BlockSpec((3, DB, hd), lambda h: (0, 0, h)),
                  pl.BlockSpec((hd, 1), lambda h: (h, 0)),
                  pl.BlockSpec((1, hd), lambda h: (0, 0)),
                  sspec],
        out_specs=(pl.BlockSpec((DB, hd), lambda h: (0, h)), sspec),
        out_shape=(jax.ShapeDtypeStruct((DB, D), BF16), jax.ShapeDtypeStruct(s0.shape, s0.dtype)),
        compiler_params=_params(("parallel",)),
        name="hgrn_step",
    )(zcols, zrows, lb.reshape(D, 1), o_norm, s0)


def _router_kernel(h_ref, w_ref, b_ref, ri_ref, rg_ref, cnt_ref, carry_ref):
    @pl.when(pl.program_id(0) == 0)
    def _():
        carry_ref[...] = jnp.zeros_like(carry_ref)

    hb = h_ref[...].astype(BF16)
    tm = hb.shape[0]
    lane = lax.broadcasted_iota(I32, (tm, V7X_LANES), 1)
    lane_f = lane.astype(F32)
    valid = lane < N_GROUPS
    sc = [_sigmoid(jnp.dot(hb, w_ref[j], preferred_element_type=F32)) for j in range(EXPERTS_PER_GROUP)]
    sel = [jnp.where(valid, sc[j] + b_ref[j], NEG_INF) for j in range(EXPERTS_PER_GROUP)]
    gs = None
    for i in range(EXPERTS_PER_GROUP):
        for j in range(i + 1, EXPERTS_PER_GROUP):
            pair = sel[i] + sel[j]
            gs = pair if gs is None else jnp.maximum(gs, pair)
    gmax = jnp.max(gs, axis=-1, keepdims=True)
    gidx = jnp.min(jnp.where(gs == gmax, lane_f, float(V7X_LANES)), axis=-1, keepdims=True)
    in_g = lane_f == gidx
    pick = lambda x: jnp.sum(jnp.where(in_g, x, 0.0), axis=-1, keepdims=True)
    cand = [pick(sel[j]) for j in range(EXPERTS_PER_GROUP)]
    csc = [pick(sc[j]) for j in range(EXPERTS_PER_GROUP)]

    def first_argmax(vals):
        best, idx = vals[0], jnp.zeros_like(vals[0])
        for j in range(1, len(vals)):
            upd = vals[j] > best
            best = jnp.where(upd, vals[j], best)
            idx = jnp.where(upd, float(j), idx)
        return idx

    i1 = first_argmax(cand)
    i2 = first_argmax([jnp.where(i1 == float(j), NEG_INF, cand[j]) for j in range(EXPERTS_PER_GROUP)])
    at = lambda idx: sum(jnp.where(idx == float(j), csc[j], 0.0) for j in range(EXPERTS_PER_GROUP))
    w1, w2 = at(i1), at(i2)
    wsum = w1 + w2
    e1 = gidx * float(EXPERTS_PER_GROUP) + i1
    e2 = gidx * float(EXPERTS_PER_GROUP) + i2

    hit1 = lane_f == e1
    hit2 = lane_f == e2
    onehot = jnp.where(hit1 | hit2, 1.0, 0.0)
    lower = _iota01((tm, tm), lambda r, c: c < r)
    prior = jnp.dot(lower, onehot.astype(BF16), preferred_element_type=F32) + carry_ref[...]
    r1 = jnp.sum(jnp.where(hit1, prior, 0.0), axis=-1, keepdims=True)
    r2 = jnp.sum(jnp.where(hit2, prior, 0.0), axis=-1, keepdims=True)
    total = carry_ref[...] + jnp.sum(onehot, axis=0, keepdims=True)
    carry_ref[...] = total
    cnt_ref[...] = total.astype(I32)

    ints = jnp.where(lane == 0, e1, jnp.where(lane == 1, e2, jnp.where(lane == 2, r1, jnp.where(lane == 3, r2, 0.0))))
    ri_ref[...] = ints.astype(I32)
    rg_ref[...] = jnp.where(lane == 0, w1 / wsum, jnp.where(lane == 1, w2 / wsum, 0.0))


def moe_route(h, router_w, router_b, tm=256):
    T, D = h.shape
    tm = min(tm, T)
    pad = V7X_LANES - N_GROUPS
    wg = router_w.reshape(D, N_GROUPS, EXPERTS_PER_GROUP).transpose(2, 0, 1)
    wg = jnp.pad(wg, ((0, 0), (0, 0), (0, pad))).astype(BF16)
    bg = jnp.pad(router_b.astype(F32).reshape(N_GROUPS, EXPERTS_PER_GROUP).T, ((0, 0), (0, pad)))
    bg = bg.reshape(EXPERTS_PER_GROUP, 1, V7X_LANES)
    row = pl.BlockSpec((tm, V7X_LANES), lambda i: (i, 0))
    ri, rg, cnt = pl.pallas_call(
        _router_kernel,
        grid=(T // tm,),
        in_specs=[pl.BlockSpec((tm, D), lambda i: (i, 0)),
                  pl.BlockSpec((EXPERTS_PER_GROUP, D, V7X_LANES), lambda i: (0, 0, 0)),
                  pl.BlockSpec((EXPERTS_PER_GROUP, 1, V7X_LANES), lambda i: (0, 0, 0))],
        out_specs=(row, row, pl.BlockSpec((1, V7X_LANES), lambda i: (0, 0))),
        out_shape=(jax.ShapeDtypeStruct((T, V7X_LANES), I32), jax.ShapeDtypeStruct((T, V7X_LANES), F32),
                   jax.ShapeDtypeStruct((1, V7X_LANES), I32)),
        scratch_shapes=[pltpu.VMEM((1, V7X_LANES), F32)],
        compiler_params=_params(("arbitrary",)),
        name="moe_route",
    )(h, wg, bg)
    return ri[:, 0:2], ri[:, 2:4], rg, cnt[0, :N_EXPERTS]


def _dispatch_kernel(dest_ref, h_ref, init_ref, xb_ref, sem, *, tm):
    del init_ref
    base = pl.program_id(0) * tm

    def row_copy(r, k):
        d = dest_ref[(base + r) * TOP_K + k]
        return pltpu.make_async_copy(h_ref.at[pl.ds(r, 1)], xb_ref.at[pl.ds(d, 1)], sem)

    def start(r, _):
        for k in range(TOP_K):
            row_copy(r, k).start()
        return 0

    def wait(r, _):
        for k in range(TOP_K):
            row_copy(r, k).wait()
        return 0

    lax.fori_loop(0, tm, start, 0)
    lax.fori_loop(0, tm, wait, 0)


def moe_dispatch(h, dest, cap, tm=256):
    T, D = h.shape
    tm = min(tm, T)
    grid_spec = pltpu.PrefetchScalarGridSpec(
        num_scalar_prefetch=1,
        grid=(T // tm,),
        in_specs=[pl.BlockSpec((tm, D), lambda i, d: (i, 0)),
                  pl.BlockSpec(memory_space=pl.ANY)],
        out_specs=pl.BlockSpec(memory_space=pl.ANY),
        scratch_shapes=[pltpu.SemaphoreType.DMA(())],
    )
    return pl.pallas_call(
        functools.partial(_dispatch_kernel, tm=tm),
        grid_spec=grid_spec,
        out_shape=jax.ShapeDtypeStruct((cap, D), F32),
        input_output_aliases={2: 0},
        compiler_params=_params(("arbitrary",)),
        name="moe_dispatch",
    )(dest, h, jnp.zeros((cap, D), F32))


def _moe_up_kernel(sb_ref, sc_ref, ib_ref, ic_ref, ie_ref, first_ref, used_ref,
                   x_ref, wg_ref, wu_ref, a_ref, wgb_ref, wub_ref):
    s = pl.program_id(0)

    @pl.when(used_ref[s] == 1)
    def _():
        @pl.when(first_ref[s] == 1)
        def _():
            wgb_ref[...] = wg_ref[...].astype(BF16)
            wub_ref[...] = wu_ref[...].astype(BF16)

        xb = x_ref[...].astype(BF16)
        g = jnp.dot(xb, wgb_ref[...], preferred_element_type=F32)
        u = jnp.dot(xb, wub_ref[...], preferred_element_type=F32)
        a_ref[...] = (_silu(g) * u).astype(BF16)

    @pl.when(used_ref[s] == 0)
    def _():
        a_ref[...] = jnp.zeros_like(a_ref)


def moe_up(xb, tables, w_gate, w_up, layer, blk, ce=512):
    cap, D = xb.shape
    DE = w_gate.shape[3]
    ce = min(ce, DE)
    nsteps = tables[0].shape[0]
    wspec = pl.BlockSpec((None, None, D, ce), lambda s, sb, sc, ib, ic, ie, fi, us: (layer, ie[s], 0, ic[s]))
    grid_spec = pltpu.PrefetchScalarGridSpec(
        num_scalar_prefetch=7,
        grid=(nsteps,),
        in_specs=[pl.BlockSpec((blk, D), lambda s, sb, sc, ib, ic, ie, fi, us: (ib[s], 0)), wspec, wspec],
        out_specs=pl.BlockSpec((blk, ce), lambda s, sb, sc, ib, ic, ie, fi, us: (sb[s], sc[s])),
        scratch_shapes=[pltpu.VMEM((D, ce), BF16), pltpu.VMEM((D, ce), BF16)],
    )
    return pl.pallas_call(
        _moe_up_kernel,
        grid_spec=grid_spec,
        out_shape=jax.ShapeDtypeStruct((cap, DE), BF16),
        compiler_params=_params(("arbitrary",)),
        name="moe_up",
    )(*tables, xb, w_gate, w_up)


def _moe_down_kernel(be_ref, ib_ref, first_ref, nu_ref, a_ref, wd_ref, y_ref, wdb_ref):
    i = pl.program_id(0)

    @pl.when(i < nu_ref[0])
    def _():
        @pl.when(first_ref[i] == 1)
        def _():
            wdb_ref[...] = wd_ref[...].astype(BF16)

        y_ref[...] = jnp.dot(a_ref[...], wdb_ref[...], preferred_element_type=F32)

    @pl.when(i >= nu_ref[0])
    def _():
        y_ref[...] = jnp.zeros_like(y_ref)


def moe_down(a, tables, w_down, layer, blk):
    cap, DE = a.shape
    D = w_down.shape[3]
    grid_spec = pltpu.PrefetchScalarGridSpec(
        num_scalar_prefetch=4,
        grid=(cap // blk,),
        in_specs=[pl.BlockSpec((blk, DE), lambda i, be, ib, fi, nu: (ib[i], 0)),
                  pl.BlockSpec((None, None, DE, D), lambda i, be, ib, fi, nu: (layer, be[i], 0, 0))],
        out_specs=pl.BlockSpec((blk, D), lambda i, be, ib, fi, nu: (i, 0)),
        scratch_shapes=[pltpu.VMEM((DE, D), BF16)],
    )
    return pl.pallas_call(
        _moe_down_kernel,
        grid_spec=grid_spec,
        out_shape=jax.ShapeDtypeStruct((cap, D), F32),
        compiler_params=_params(("arbitrary",)),
        name="moe_down",
    )(*tables, a, w_down)


def _moe_tables(counts, blk, nblk, nch):
    nb = (counts + blk - 1) // blk
    bend = jnp.cumsum(nb)
    bstart = bend - nb
    nu = bend[-1]
    blocks = jnp.arange(nblk, dtype=I32)
    be = jnp.minimum(jnp.searchsorted(bend, blocks, side="right"), N_EXPERTS - 1).astype(I32)
    ib2 = jnp.minimum(blocks, nu - 1)
    be2 = be[ib2]
    first2 = ((blocks == 0) | (be2 != jnp.roll(be2, 1))).astype(I32)
    down = (be2, ib2, first2, nu.reshape(1).astype(I32))
    used_b = blocks < nu
    start_b = jnp.where(used_b, bstart[be], blocks)
    nb_b = jnp.where(used_b, nb[be], 1)
    chunks = jnp.arange(nch, dtype=I32)
    pos = start_b[:, None] * nch + chunks[None, :] * nb_b[:, None] + (blocks - start_b)[:, None]
    nsteps = nblk * nch
    sb = jnp.zeros((nsteps,), I32).at[pos.reshape(-1)].set(jnp.repeat(blocks, nch))
    sc = jnp.zeros((nsteps,), I32).at[pos.reshape(-1)].set(jnp.tile(chunks, nblk))
    steps = jnp.arange(nsteps, dtype=I32)
    used = (steps < nu * nch).astype(I32)
    clamp = jnp.minimum(steps, nu * nch - 1)
    ib, ic = sb[clamp], sc[clamp]
    ie = be[ib]
    first = ((steps == 0) | (ie != jnp.roll(ie, 1)) | (ic != jnp.roll(ic, 1))).astype(I32)
    up = (sb, sc, ib, ic, ie, first, used)
    return bstart * blk, up, down


def _combine_kernel(dest_ref, yb_ref, x_ref, gate_ref, rg_ref, o_ref, ybuf_ref, sem, *, tm):
    base = pl.program_id(0) * tm

    def row_copy(r, k):
        d = dest_ref[(base + r) * TOP_K + k]
        return pltpu.make_async_copy(yb_ref.at[pl.ds(d, 1)], ybuf_ref.at[k, pl.ds(r, 1)], sem)

    def start(r, _):
        for k in range(TOP_K):
            row_copy(r, k).start()
        return 0

    def wait(r, _):
        for k in range(TOP_K):
            row_copy(r, k).wait()
        return 0

    lax.fori_loop(0, tm, start, 0)
    lax.fori_loop(0, tm, wait, 0)
    rg = rg_ref[...]
    y = ybuf_ref[0] * rg[:, 0:1] + ybuf_ref[1] * rg[:, 1:2]
    o_ref[...] = x_ref[...] + gate_ref[...] * y


def moe_combine(yb, dest, x, gate, rg, rows_per_batch, tm=256):
    T, D = x.shape
    tm = min(tm, T)
    if gate.shape[1] == 1:
        rpt = rows_per_batch // tm
        gspec = pl.BlockSpec((None, 1, D), lambda i, d: (i // rpt, 0, 0))
    else:
        gspec = pl.BlockSpec((None, tm, D), lambda i, d: (0, i, 0))
    grid_spec = pltpu.PrefetchScalarGridSpec(
        num_scalar_prefetch=1,
        grid=(T // tm,),
        in_specs=[pl.BlockSpec(memory_space=pl.ANY),
                  pl.BlockSpec((tm, D), lambda i, d: (i, 0)),
                  gspec,
                  pl.BlockSpec((tm, V7X_LANES), lambda i, d: (i, 0))],
        out_specs=pl.BlockSpec((tm, D), lambda i, d: (i, 0)),
        scratch_shapes=[pltpu.VMEM((TOP_K, tm, D), F32), pltpu.SemaphoreType.DMA(())],
    )
    return pl.pallas_call(
        functools.partial(_combine_kernel, tm=tm),
        grid_spec=grid_spec,
        out_shape=jax.ShapeDtypeStruct((T, D), F32),
        compiler_params=_params(("arbitrary",)),
        name="moe_combine",
    )(dest, yb, x, gate, rg)


def moe_sublayer(h, x, gate, rows_per_batch, router_w, router_b, w_gate, w_up, w_down, layer, blk, ce=512):
    T, D = h.shape
    DE = w_gate.shape[3]
    ce = min(ce, DE)
    e_idx, rank, rg, counts = moe_route(h, router_w, router_b, tm=min(256, T))
    A = T * TOP_K
    cap = -(-(A + N_EXPERTS * (blk - 1)) // blk) * blk
    pad_start, up_tables, down_tables = _moe_tables(counts, blk, cap // blk, DE // ce)
    dest = (pad_start[e_idx] + rank).reshape(-1).astype(I32)
    xb = moe_dispatch(h, dest, cap, tm=min(256, T))
    a = moe_up(xb, up_tables, w_gate, w_up, layer, blk, ce)
    yb = moe_down(a, down_tables, w_down, layer, blk)
    return moe_combine(yb, dest, x, gate, rg, rows_per_batch, tm=min(256, T))


def kernel(x_prompt, x_sample, c_prompt, c_sample, cache_k, cache_v, cache_logf, state_hgrn, page_table,
           ada_w, ada_b, norm_g, att_w_in, att_b_f, att_q_norm, att_k_norm, att_w_o,
           rec_w_in, rec_lb_logits, rec_o_norm, rec_w_o, router_w, router_b,
           moe_w_gate, moe_w_up, moe_w_down):
    B, L, D = x_prompt.shape
    DB, LS, _ = x_sample.shape
    depth = ada_w.shape[0]
    H = ATT_HEADS
    hd = D // H
    T = B * L
    TS = DB * LS
    assert LS == 1
    P = cache_k.shape[2]

    p_lb = jax.nn.softmax(rec_lb_logits.astype(F32), axis=0)
    lb_table = jnp.cumsum(p_lb, axis=0) - p_lb[0]

    n_c = B + DB
    r_c = -(-n_c // V7X_SUBLANES) * V7X_SUBLANES
    c_all = jnp.pad(jnp.concatenate([c_prompt, c_sample], axis=0), ((0, r_c - n_c), (0, 0)))
    mod = ada_modulation(c_all, ada_w.reshape(depth * 2, D, 3 * D), ada_b.reshape(depth * 2, 1, 3 * D))
    mod = mod.reshape(depth, 2, r_c, 3, D)

    def mods(layer, sub):
        m = mod[layer, sub]
        shift, scale, gate = m[:, 0], m[:, 1], m[:, 2]
        pr = lambda a: a[:B].reshape(B, 1, D)
        sa = lambda a: a[B:n_c].reshape(1, DB, D)
        return (pr(shift), pr(scale), pr(gate)), (sa(shift), sa(scale), sa(gate))

    xp = x_prompt.reshape(T, D)
    xs = x_sample.reshape(TS, D)
    outs = {n: [] for n in ("kp", "vp", "lfp", "ks", "vs", "lfs", "sp", "ss")}
    blk_p = max(16, min(256, T * TOP_K // N_EXPERTS))
    blk_s = max(16, min(256, TS * TOP_K // N_EXPERTS))
    n_pool = cache_k.shape[1]
    ck = cache_k.reshape(cache_k.shape[0], n_pool, P * H, hd)
    cv = cache_v.reshape(cache_v.shape[0], n_pool, P * H, hd)

    for layer in range(depth):
        j = layer // 2
        (shp, scp, gp), (shs, scs, gs) = mods(layer, 0)
        g0 = norm_g[layer, 0].reshape(1, D)
        hp = norm_modulate(xp.reshape(B, L, D), g0, scp, shp).reshape(T, D)
        hs = norm_modulate(xs.reshape(1, TS, D), g0, scs, shs).reshape(TS, D)
        if layer % 2 == 0:
            w_in = att_w_in
            w_f = att_w_in[j][:, 3 * D:].reshape(1, D, H)
            b_f = att_b_f[j].reshape(1, H)
            qg = att_q_norm[j].reshape(1, hd)
            kg = att_k_norm[j].reshape(1, hd)
            q_bf = project(hp, w_in, layer=j, mode="headnorm_bf", col_off=0, n_out=D, extra=(qg,))
            k_f, k_bf = project(hp, w_in, layer=j, mode="headnorm", col_off=D, n_out=D, extra=(kg,))
            v_f, v_bf = project(hp, w_in, layer=j, mode="dual", col_off=2 * D, n_out=D)
            lf = project(hp, w_f, mode="logsig", extra=(b_f,)).reshape(B, L, H)
            cum = seq_cumsum(lf)
            cum_t = jnp.transpose(cum, (0, 2, 1)).reshape(B, H, 1, L)
            op = fox_prompt_attention(q_bf.reshape(B, L, D), k_bf.reshape(B, L, D), v_bf.reshape(B, L, D),
                                      cum, cum_t, H)
            xp = project(op.reshape(T, D), att_w_o, layer=j, mode="resid", extra=(xp, gp), rows_per_batch=L)
            outs["kp"].append(k_f.reshape(B, L, H, hd))
            outs["vp"].append(v_f.reshape(B, L, H, hd))
            outs["lfp"].append(lf)
            qs, _ = project(hs, w_in, layer=j, mode="headnorm", col_off=0, n_out=D, extra=(qg,))
            ks, _ = project(hs, w_in, layer=j, mode="headnorm", col_off=D, n_out=D, extra=(kg,))
            vs = project(hs, w_in, layer=j, mode="plain", col_off=2 * D, n_out=D)
            lfs = project(hs, w_f, mode="logsig", extra=(b_f,))
            bias = fox_sample_bias(page_table, cache_logf, j, lfs.reshape(DB, 1, H))
            osm = fox_sample_attention(page_table, qs.reshape(DB, H, hd), ks.reshape(DB, H, hd),
                                       vs.reshape(DB, H, hd), ck, cv, j,
                                       bias.reshape(DB, bias.shape[1], 1, P * H))
            xs = project(osm.reshape(TS, D), att_w_o, layer=j, mode="resid", extra=(xs, gs))
            outs["ks"].append(ks.reshape(DB, LS, H, hd))
            outs["vs"].append(vs.reshape(DB, LS, H, hd))
            outs["lfs"].append(lfs.reshape(DB, LS, H))
        else:
            lb = lb_table[layer].reshape(1, D)
            on = rec_o_norm[j].reshape(1, hd)
            zp = project(hp, rec_w_in, layer=j, mode="plain")
            s0p = jnp.zeros((B, REC_HEADS, hd, hd), state_hgrn.dtype)
            op, sp = hgrn_prompt(zp.reshape(B, L, 4 * D), lb, on, s0p, REC_HEADS)
            xp = project(op.reshape(T, D), rec_w_o, layer=j, mode="resid", extra=(xp, gp), rows_per_batch=L)
            zs = project(hs, rec_w_in, layer=j, mode="plain")
            osm, ss = hgrn_step(zs, lb_table[layer], on, state_hgrn[j], REC_HEADS)
            xs = project(osm, rec_w_o, layer=j, mode="resid", extra=(xs, gs))
            outs["sp"].append(sp)
            outs["ss"].append(ss)

        (shp, scp, gp), (shs, scs, gs) = mods(layer, 1)
        g1 = norm_g[layer, 1].reshape(1, D)
        hp = norm_modulate(xp.reshape(B, L, D), g1, scp, shp, out_dtype=F32).reshape(T, D)
        hs = norm_modulate(xs.reshape(1, TS, D), g1, scs, shs, out_dtype=F32).reshape(TS, D)
        moe_w = (router_w, router_b, moe_w_gate, moe_w_up, moe_w_down, layer)
        xp = moe_sublayer(hp, xp, gp, L, *moe_w, blk=blk_p)
        xs = moe_sublayer(hs, xs, gs, None, *moe_w, blk=blk_s)

    st = lambda n: jnp.stack(outs[n])
    return (xp.reshape(B, L, D), xs.reshape(DB, LS, D), st("kp"), st("vp"), st("lfp"), st("ks"), st("vs"),
            st("lfs"), st("sp"), st("ss"))
```

```python
import functools

import jax
import jax.numpy as jnp
from jax import lax
from jax.experimental import pallas as pl
from jax.experimental.pallas import tpu as pltpu

F32 = jnp.float32
BF16 = jnp.bfloat16
I32 = jnp.int32
EPS = 1e-6

V7X_LANES = 128
V7X_SUBLANES = 8
V7X_VMEM_LIMIT_BYTES = 56 * 1024 * 1024

ATT_HEADS = 32
REC_HEADS = 32
N_GROUPS = 8
EXPERTS_PER_GROUP = 4
N_EXPERTS = N_GROUPS * EXPERTS_PER_GROUP
TOP_K = 2
REC_SUB = 16
NEG_INF = float("-inf")
LOG2E = 1.4426950408889634
CAST_ROWS = 256


def _params(sem):
    return pltpu.CompilerParams(dimension_semantics=sem, vmem_limit_bytes=V7X_VMEM_LIMIT_BYTES)


def _split3(x):
    x1 = x.astype(BF16)
    r1 = x - x1.astype(F32)
    x2 = r1.astype(BF16)
    x3 = (r1 - x2.astype(F32)).astype(BF16)
    return x1, x2, x3


def _dot01(m01, x):
    x1, x2, x3 = _split3(x)
    d = lambda b: jnp.dot(m01, b, preferred_element_type=F32)
    return d(x1) + (d(x2) + d(x3))


def _iota01(shape, pred):
    r = lax.broadcasted_iota(I32, shape, 0)
    c = lax.broadcasted_iota(I32, shape, 1)
    return jnp.where(pred(r, c), 1.0, 0.0).astype(BF16)


def _sigmoid(x):
    return 1.0 / (1.0 + jnp.exp(-x))


def _silu(x):
    return x * _sigmoid(x)


def _log_sigmoid(x):
    return jnp.minimum(x, 0.0) - jnp.log1p(jnp.exp(-jnp.abs(x)))


def _logaddexp(a, b):
    return jnp.maximum(a, b) + jnp.log1p(jnp.exp(-jnp.abs(a - b)))


def _ada_kernel(c_ref, w_ref, b_ref, o_ref):
    sc = _silu(c_ref[...]).astype(BF16)
    o_ref[...] = jnp.dot(sc, w_ref[...].astype(BF16), preferred_element_type=F32) + b_ref[...]


def ada_modulation(c_all, ada_w, ada_b, tn=512):
    S, D, N = ada_w.shape
    R = c_all.shape[0]
    tn = min(tn, N)
    return pl.pallas_call(
        _ada_kernel,
        grid=(S, N // tn),
        in_specs=[pl.BlockSpec((R, D), lambda s, j: (0, 0)),
                  pl.BlockSpec((None, D, tn), lambda s, j: (s, 0, j)),
                  pl.BlockSpec((None, 1, tn), lambda s, j: (s, 0, j))],
        out_specs=pl.BlockSpec((None, R, tn), lambda s, j: (s, 0, j)),
        out_shape=jax.ShapeDtypeStruct((S, R, N), F32),
        compiler_params=_params(("parallel", "parallel")),
        name="ada_modulation",
    )(c_all, ada_w, ada_b)


def _norm_mod_kernel(x_ref, g_ref, scale_ref, shift_ref, o_ref):
    x = x_ref[...]
    y = x * lax.rsqrt(jnp.mean(x * x, axis=-1, keepdims=True) + EPS)
    h = (y * g_ref[...]) * (1.0 + scale_ref[...]) + shift_ref[...]
    o_ref[...] = h.astype(o_ref.dtype)


def norm_modulate(x, g, scale, shift, out_dtype=BF16, tr=256):
    NB, L, D = x.shape
    LS = scale.shape[1]
    tr = min(tr, L)
    ts = tr if LS == L else 1
    smap = (lambda b, i: (b, i, 0)) if LS == L else (lambda b, i: (b, 0, 0))
    return pl.pallas_call(
        _norm_mod_kernel,
        grid=(NB, L // tr),
        in_specs=[pl.BlockSpec((None, tr, D), lambda b, i: (b, i, 0)),
                  pl.BlockSpec((1, D), lambda b, i: (0, 0)),
                  pl.BlockSpec((None, ts, D), smap),
                  pl.BlockSpec((None, ts, D), smap)],
        out_specs=pl.BlockSpec((None, tr, D), lambda b, i: (b, i, 0)),
        out_shape=jax.ShapeDtypeStruct((NB, L, D), out_dtype),
        compiler_params=_params(("parallel", "parallel")),
        name="norm_modulate",
    )(x, g, scale, shift)


def _head_rms(acc, g):
    parts = []
    for c in range(acc.shape[1] // V7X_LANES):
        blk = acc[:, c * V7X_LANES:(c + 1) * V7X_LANES]
        y = blk * lax.rsqrt(jnp.mean(blk * blk, axis=-1, keepdims=True) + EPS)
        parts.append(y * g)
    return jnp.concatenate(parts, axis=-1) if len(parts) > 1 else parts[0]


def _proj_kernel(*refs, mode):
    a_ref, w_ref = refs[0], refs[1]
    wbf_ref = refs[-1]

    @pl.when(pl.program_id(1) == 0)
    def _():
        wbf_ref[...] = w_ref[...].astype(BF16)

    acc = jnp.dot(a_ref[...], wbf_ref[...], preferred_element_type=F32)
    if mode == "plain":
        refs[2][...] = acc
    elif mode == "headnorm":
        y = _head_rms(acc, refs[2][...])
        refs[3][...] = y
        refs[4][...] = y.astype(BF16)
    elif mode == "headnorm_bf":
        refs[3][...] = _head_rms(acc, refs[2][...]).astype(BF16)
    elif mode == "dual":
        refs[2][...] = acc
        refs[3][...] = acc.astype(BF16)
    elif mode == "logsig":
        refs[3][...] = _log_sigmoid(acc + refs[2][...])
    elif mode == "resid":
        refs[4][...] = refs[2][...] + refs[3][...] * acc
    else:
        raise ValueError(mode)


def project(a, w, *, mode, layer=0, col_off=0, n_out=None, extra=(), rows_per_batch=None, tm=1024, tn=512):
    M, K = a.shape
    n_out = w.shape[2] - col_off if n_out is None else n_out
    tm = min(tm, M)
    if mode == "resid" and extra[1].shape[1] == 1:
        tm = min(tm, rows_per_batch)
    tn = min(tn, n_out)
    assert M % tm == 0 and n_out % tn == 0 and col_off % tn == 0
    joff = col_off // tn
    grid = (n_out // tn, M // tm)
    in_specs = [pl.BlockSpec((tm, K), lambda j, i: (i, 0)),
                pl.BlockSpec((None, K, tn), lambda j, i: (layer, 0, j + joff))]
    o_spec = pl.BlockSpec((tm, tn), lambda j, i: (i, j))
    f32_out = jax.ShapeDtypeStruct((M, n_out), F32)
    bf_out = jax.ShapeDtypeStruct((M, n_out), BF16)
    if mode == "plain":
        out_specs, out_shape = o_spec, f32_out
    elif mode == "headnorm":
        in_specs.append(pl.BlockSpec((1, V7X_LANES), lambda j, i: (0, 0)))
        out_specs, out_shape = (o_spec, o_spec), (f32_out, bf_out)
    elif mode == "headnorm_bf":
        in_specs.append(pl.BlockSpec((1, V7X_LANES), lambda j, i: (0, 0)))
        out_specs, out_shape = o_spec, bf_out
    elif mode == "dual":
        out_specs, out_shape = (o_spec, o_spec), (f32_out, bf_out)
    elif mode == "logsig":
        in_specs.append(pl.BlockSpec((1, tn), lambda j, i: (0, j)))
        out_specs, out_shape = o_spec, f32_out
    elif mode == "resid":
        in_specs.append(o_spec)
        if extra[1].shape[1] == 1:
            assert rows_per_batch % tm == 0
            rpt = rows_per_batch // tm
            in_specs.append(pl.BlockSpec((None, 1, tn), lambda j, i: (i // rpt, 0, j)))
        else:
            assert extra[1].shape[:2] == (1, M)
            in_specs.append(pl.BlockSpec((None, tm, tn), lambda j, i: (0, i, j)))
        out_specs, out_shape = o_spec, f32_out
    else:
        raise ValueError(mode)
    return pl.pallas_call(
        functools.partial(_proj_kernel, mode=mode),
        grid=grid,
        in_specs=in_specs,
        out_specs=out_specs,
        out_shape=out_shape,
        scratch_shapes=[pltpu.VMEM((K, tn), BF16)],
        compiler_params=_params(("parallel", "arbitrary")),
        name="project_" + mode,
    )(a, w, *extra)


def _cumsum_kernel(x_ref, o_ref, carry_ref):
    @pl.when(pl.program_id(1) == 0)
    def _():
        carry_ref[...] = jnp.zeros_like(carry_ref)

    x = x_ref[...]
    n = x.shape[0]
    tril = _iota01((n, n), lambda r, c: c <= r)
    cs = _dot01(tril, x) + carry_ref[...]
    o_ref[...] = cs
    carry_ref[...] = cs[n - 1:n, :]


def seq_cumsum(x, tc=256):
    B, L, H = x.shape
    tc = min(tc, L)
    return pl.pallas_call(
        _cumsum_kernel,
        grid=(B, L // tc),
        in_specs=[pl.BlockSpec((None, tc, H), lambda b, i: (b, i, 0))],
        out_specs=pl.BlockSpec((None, tc, H), lambda b, i: (b, i, 0)),
        out_shape=jax.ShapeDtypeStruct((B, L, H), F32),
        scratch_shapes=[pltpu.VMEM((1, H), F32)],
        compiler_params=_params(("parallel", "arbitrary")),
        name="seq_cumsum",
    )(x)


def _fox_prompt_kernel(q_ref, k_ref, v_ref, cq_ref, ck_ref, o_ref, *, tq, tk, scale):
    i = pl.program_id(2)
    h = pl.program_id(1)
    q = q_ref[...]
    cq_all = cq_ref[...]
    lane = lax.broadcasted_iota(I32, cq_all.shape, 1)
    cq = jnp.sum(jnp.where(lane == h, cq_all, 0.0), axis=-1, keepdims=True)
    hd = q.shape[-1]

    def step(j, carry, masked):
        m, l, acc = carry
        ks = pl.multiple_of(j * tk, tk)
        kb = k_ref[pl.ds(ks, tk), :]
        vb = v_ref[pl.ds(ks, tk), :]
        s = lax.dot_general(q, kb, (((1,), (1,)), ((), ())), preferred_element_type=F32) * scale
        s = s + (cq - ck_ref[:, pl.ds(ks, tk)])
        if masked:
            qpos = i * tq + lax.broadcasted_iota(I32, (tq, tk), 0)
            kpos = j * tk + lax.broadcasted_iota(I32, (tq, tk), 1)
            s = jnp.where(kpos <= qpos, s, NEG_INF)
        m_new = jnp.maximum(m, jnp.max(s, axis=-1, keepdims=True))
        a = jnp.exp(m - m_new)
        p = jnp.exp(s - m_new)
        l = l * a + jnp.sum(p, axis=-1, keepdims=True)
        acc = acc * a + jnp.dot(p.astype(BF16), vb, preferred_element_type=F32)
        return m_new, l, acc

    init = (jnp.full((tq, 1), NEG_INF, F32), jnp.zeros((tq, 1), F32), jnp.zeros((tq, hd), F32))
    nfull = (i * tq) // tk
    carry = lax.fori_loop(0, nfull, lambda j, c: step(j, c, False), init)
    for d in range(tq // tk):
        carry = step(nfull + d, carry, True)
    m, l, acc = carry
    o_ref[...] = (acc / l).astype(o_ref.dtype)


def fox_prompt_attention(q, k, v, cum, cum_t, n_heads, tq=512, tk=512):
    B, L, D = q.shape
    hd = D // n_heads
    tq = min(tq, L)
    tk = min(tk, tq)
    assert L % tq == 0 and tq % tk == 0
    kern = functools.partial(_fox_prompt_kernel, tq=tq, tk=tk, scale=hd ** -0.5)
    return pl.pallas_call(
        kern,
        grid=(B, n_heads, L // tq),
        in_specs=[pl.BlockSpec((None, tq, hd), lambda b, h, i: (b, i, h)),
                  pl.BlockSpec((None, L, hd), lambda b, h, i: (b, 0, h)),
                  pl.BlockSpec((None, L, hd), lambda b, h, i: (b, 0, h)),
                  pl.BlockSpec((None, tq, n_heads), lambda b, h, i: (b, i, 0)),
                  pl.BlockSpec((None, None, 1, L), lambda b, h, i: (b, h, 0, 0))],
        out_specs=pl.BlockSpec((None, tq, hd), lambda b, h, i: (b, i, h)),
        out_shape=jax.ShapeDtypeStruct((B, L, D), BF16),
        compiler_params=_params(("parallel", "parallel", "arbitrary")),
        name="fox_prompt_attention",
    )(q, k, v, cum, cum_t)


def _fox_tail_kernel(pt_ref, *refs, n_seq):
    lf_refs, (cn_ref, o_ref, carry_ref) = refs[:n_seq], refs[n_seq:]

    @pl.when(pl.program_id(0) == 0)
    def _():
        carry_ref[...] = jnp.zeros_like(carry_ref)

    n = lf_refs[0].shape[0]
    upper = _iota01((n, n), lambda r, c: c > r)
    for b in range(n_seq):
        lf = lf_refs[b][...]
        tl = _dot01(upper, lf) + carry_ref[b]
        o_ref[b] = tl + cn_ref[b]
        carry_ref[b] = tl[0:1, :] + lf[0:1, :]


def fox_sample_bias(page_table, cache_logf, layer, cn):
    DB, NP = page_table.shape
    _, _, P, H = cache_logf.shape
    page = lambda b: pl.BlockSpec((None, None, P, H), lambda p, pt: (layer, pt[b * NP + NP - 1 - p], 0, 0))
    grid_spec = pltpu.PrefetchScalarGridSpec(
        num_scalar_prefetch=1,
        grid=(NP,),
        in_specs=[page(b) for b in range(DB)] + [pl.BlockSpec((DB, 1, H), lambda p, pt: (0, 0, 0))],
        out_specs=pl.BlockSpec((DB, None, P, H), lambda p, pt: (0, NP - 1 - p, 0, 0)),
        scratch_shapes=[pltpu.VMEM((DB, 1, H), F32)],
    )
    return pl.pallas_call(
        functools.partial(_fox_tail_kernel, n_seq=DB),
        grid_spec=grid_spec,
        out_shape=jax.ShapeDtypeStruct((DB, NP, P, H), F32),
        compiler_params=_params(("arbitrary",)),
        name="fox_sample_bias",
    )(page_table.reshape(-1), *([cache_logf] * DB), cn)


def _fox_sample_kernel(pt_ref, q_ref, kn_ref, vn_ref, *refs, scale, n_heads, pps):
    ck_refs, cv_refs = refs[:pps], refs[pps:2 * pps]
    bias_ref, o_ref, m_ref, l_ref, acc_ref, mask_ref = refs[2 * pps:]
    p = pl.program_id(1)

    @pl.when(p == 0)
    def _():
        m_ref[...] = jnp.full_like(m_ref, NEG_INF)
        l_ref[...] = jnp.zeros_like(l_ref)
        acc_ref[...] = jnp.zeros_like(acc_ref)
        row = lax.broadcasted_iota(I32, mask_ref.shape, 0)
        col = lax.broadcasted_iota(I32, mask_ref.shape, 1)
        mask_ref[...] = jnp.where((col & (n_heads - 1)) == row, 0.0, NEG_INF)

    q = q_ref[...]
    qb = q.astype(BF16)
    ss = []
    for i in range(pps):
        kb = ck_refs[i][...].astype(BF16)
        s = lax.dot_general(qb, kb, (((1,), (1,)), ((), ())), preferred_element_type=F32) * scale
        ss.append(s + bias_ref[i] + mask_ref[...])
    m = m_ref[...]
    m_new = m
    for s in ss:
        m_new = jnp.maximum(m_new, jnp.max(s, axis=-1, keepdims=True))
    a = jnp.exp(m - m_new)
    l = l_ref[...] * a
    acc = acc_ref[...] * a
    for i in range(pps):
        pe = jnp.exp(ss[i] - m_new)
        l = l + jnp.sum(pe, axis=-1, keepdims=True)
        acc = acc + jnp.dot(pe.astype(BF16), cv_refs[i][...].astype(BF16), preferred_element_type=F32)
    m_ref[...] = m_new
    l_ref[...] = l
    acc_ref[...] = acc

    @pl.when(p == pl.num_programs(1) - 1)
    def _():
        s_self = jnp.sum(q * kn_ref[...], axis=-1, keepdims=True) * scale
        m_fin = jnp.maximum(m_new, s_self)
        a_fin = jnp.exp(m_new - m_fin)
        p_self = jnp.exp(s_self - m_fin)
        l_fin = l * a_fin + p_self
        acc_fin = acc * a_fin + p_self * vn_ref[...]
        o_ref[...] = (acc_fin / l_fin).astype(o_ref.dtype)


def fox_sample_attention(page_table, q, k_new, v_new, cache_k, cache_v, layer, bias, pps=4):
    DB, NP = page_table.shape
    _, H, hd = q.shape
    PH = cache_k.shape[2]
    pps = min(pps, NP)
    assert H & (H - 1) == 0 and NP % pps == 0
    tok = pl.BlockSpec((None, H, hd), lambda b, p, pt: (b, 0, 0))
    page = lambda i: pl.BlockSpec((None, None, PH, hd), lambda b, p, pt: (layer, pt[b * NP + p * pps + i], 0, 0))
    pages = [page(i) for i in range(pps)]
    grid_spec = pltpu.PrefetchScalarGridSpec(
        num_scalar_prefetch=1,
        grid=(DB, NP // pps),
        in_specs=[tok, tok, tok] + pages + pages +
                 [pl.BlockSpec((None, pps, 1, PH), lambda b, p, pt: (b, p, 0, 0))],
        out_specs=tok,
        scratch_shapes=[pltpu.VMEM((H, 1), F32), pltpu.VMEM((H, 1), F32), pltpu.VMEM((H, hd), F32),
                        pltpu.VMEM((H, PH), F32)],
    )
    return pl.pallas_call(
        functools.partial(_fox_sample_kernel, scale=hd ** -0.5, n_heads=H, pps=pps),
        grid_spec=grid_spec,
        out_shape=jax.ShapeDtypeStruct((DB, H, hd), BF16),
        compiler_params=_params(("parallel", "arbitrary")),
        name="fox_sample_attention",
    )(page_table.reshape(-1), q, k_new, v_new, *([cache_k] * pps), *([cache_v] * pps), bias)


def _hgrn_gates(zq, zf, lb):
    qt = _silu(zq)
    g = _logaddexp(jnp.log(lb), jnp.log1p(-lb) + _log_sigmoid(zf))
    kk = (1.0 - lb) * _sigmoid(-zf)
    return qt, kk, g


def _hgrn_prompt_kernel(zq_ref, zf_ref, zi_ref, zg_ref, lb_ref, on_ref, s0_ref, o_ref, sout_ref,
                        st_ref, k_scr, b2_scr, kv_scr, oi_scr, *, tb, hp):
    hd = on_ref.shape[-1]
    for j in range(hp):
        lanes = slice(j * hd, (j + 1) * hd)
        _hgrn_prompt_head(zq_ref.at[:, lanes], zf_ref.at[:, lanes], zi_ref.at[:, lanes], zg_ref.at[:, lanes],
                          lb_ref.at[:, lanes], on_ref, s0_ref.at[j], o_ref.at[:, lanes], sout_ref.at[j],
                          st_ref.at[j], k_scr.at[j], b2_scr.at[j], kv_scr.at[j], oi_scr.at[j], tb=tb)


def _hgrn_prompt_head(zq_ref, zf_ref, zi_ref, zg_ref, lb_ref, on_ref, s0_ref, o_ref, sout_ref,
                      st_ref, k_scr, b2_scr, kv_scr, oi_scr, *, tb):
    t = pl.program_id(2)
    C = REC_SUB
    nc = tb // C
    half = V7X_SUBLANES

    @pl.when(t == 0)
    def _():
        st_ref[...] = s0_ref[...].T

    qt, kk, g = _hgrn_gates(zq_ref[...], zf_ref[...], lb_ref[...])
    kdim = qt.shape[-1]
    sh = C.bit_length() - 1
    b = _dot01(_iota01((tb, tb), lambda r, c: ((r >> sh) == (c >> sh)) & (c <= r)), g)
    b2 = b * LOG2E
    k_scr[...] = kk
    b2_scr[...] = b2
    vt = zi_ref[...].T.astype(BF16)

    for c in range(nc):
        rows = slice(c * C, (c + 1) * C)
        btot = b[c * C + C - 1:(c + 1) * C, :]
        ke = (kk[rows, :] * jnp.exp(btot - b[rows, :])).astype(BF16)
        kv_scr[c] = jnp.dot(vt[:, rows], ke, preferred_element_type=F32)

    st = st_ref[...]
    for c in range(nc):
        rows = slice(c * C, (c + 1) * C)
        qe = (qt[rows, :] * jnp.exp(b[rows, :])).astype(BF16)
        oi_scr[rows, :] = lax.dot_general(qe, st.astype(BF16), (((1,), (1,)), ((), ())),
                                          preferred_element_type=F32)
        st = st * jnp.exp(b[c * C + C - 1:(c + 1) * C, :]) + kv_scr[c]
    st_ref[...] = st

    o_parts = []
    for c in range(nc):
        r0 = c * C
        for hh in range(C // half):
            h0 = r0 + hh * half
            qh = qt[h0:h0 + half, :]
            bh = b2[h0:h0 + half, :]
            tau = hh * half + lax.broadcasted_iota(I32, (half, kdim), 0)
            acc = jnp.zeros((half, kdim), F32)
            for sig in range(min(C, (hh + 1) * half)):
                d = bh - b2_scr[pl.ds(r0 + sig, 1), :]
                if sig > hh * half:
                    d = jnp.where(tau >= sig, d, NEG_INF)
                a = jnp.sum(qh * k_scr[pl.ds(r0 + sig, 1), :] * jnp.exp2(d), axis=-1, keepdims=True)
                acc = acc + a * zi_ref[pl.ds(r0 + sig, 1), :]
            o_parts.append(acc)

    o = jnp.concatenate(o_parts, axis=0) + oi_scr[...]
    y = o * lax.rsqrt(jnp.mean(o * o, axis=-1, keepdims=True) + EPS) * on_ref[...]
    o_ref[...] = (y * _silu(zg_ref[...])).astype(o_ref.dtype)

    @pl.when(t == pl.num_programs(2) - 1)
    def _():
        sout_ref[...] = st.T


def hgrn_prompt(z, lb, o_norm, s0, n_heads, tb=256, hp=2):
    B, L, D4 = z.shape
    D = D4 // 4
    hd = D // n_heads
    tb = min(tb, L)
    hp = min(hp, n_heads)
    assert L % tb == 0 and tb % REC_SUB == 0 and n_heads % hp == 0
    ng = n_heads // hp
    zspec = lambda part: pl.BlockSpec((None, tb, hp * hd), lambda b, h, t: (b, t, part * ng + h))
    sspec = pl.BlockSpec((None, hp, hd, hd), lambda b, h, t: (b, h, 0, 0))
    return pl.pallas_call(
        functools.partial(_hgrn_prompt_kernel, tb=tb, hp=hp),
        grid=(B, ng, L // tb),
        in_specs=[zspec(0), zspec(1), zspec(2), zspec(3),
                  pl.BlockSpec((1, hp * hd), lambda b, h, t: (0, h)),
                  pl.BlockSpec((1, hd), lambda b, h, t: (0, 0)),
                  sspec],
        out_specs=(pl.BlockSpec((None, tb, hp * hd), lambda b, h, t: (b, t, h)), sspec),
        out_shape=(jax.ShapeDtypeStruct((B, L, D), BF16), jax.ShapeDtypeStruct(s0.shape, s0.dtype)),
        scratch_shapes=[pltpu.VMEM((hp, hd, hd), F32), pltpu.VMEM((hp, tb, hd), F32), pltpu.VMEM((hp, tb, hd), F32),
                        pltpu.VMEM((hp, tb // REC_SUB, hd, hd), F32), pltpu.VMEM((hp, tb, hd), F32)],
        compiler_params=_params(("parallel", "parallel", "arbitrary")),
        name="hgrn_prompt",
    )(z, z, z, z, lb, o_norm, s0)


def _hgrn_step_kernel(zc_ref, zg_ref, lbc_ref, on_ref, s0_ref, o_ref, sout_ref, *, n_seq):
    lbc = lbc_ref[...]
    rows = []
    for b in range(n_seq):
        col = lambda part: zc_ref[part, :, b:b + 1]
        qt, kk, g = _hgrn_gates(col(0), col(1), lbc)
        vrow = zg_ref[1, b:b + 1, :]
        s_new = jnp.exp(g) * s0_ref[b] + kk * vrow
        sout_ref[b] = s_new
        rows.append(jnp.sum(qt * s_new, axis=0, keepdims=True))
    o = jnp.concatenate(rows, axis=0)
    y = o * lax.rsqrt(jnp.mean(o * o, axis=-1, keepdims=True) + EPS) * on_ref[...]
    o_ref[...] = (y * _silu(zg_ref[2])).astype(o_ref.dtype)


def hgrn_step(z, lb, o_norm, s0, n_heads):
    DB, D4 = z.shape
    D = D4 // 4
    hd = D // n_heads
    z4 = z.reshape(DB, 4, D)
    zcols = jnp.transpose(z4[:, :2, :], (1, 2, 0))
    zrows = jnp.transpose(z4[:, 1:, :], (1, 0, 2))
    sspec = pl.BlockSpec((DB, None, hd, hd), lambda h: (0, h, 0, 0))
    return pl.pallas_call(
        functools.partial(_hgrn_step_kernel, n_seq=DB),
        grid=(n_heads,),
        in_specs=[pl.BlockSpec((2, hd, DB), lambda h: (0, h, 0)),
                  pl.BlockSpec((3, DB, hd), lambda h: (0, 0, h)),
                  pl.BlockSpec((hd, 1), lambda h: (h, 0)),
                  pl.BlockSpec((1, hd), lambda h: (0, 0)),
                  sspec],
        out_specs=(pl.BlockSpec((DB, hd), lambda h: (0, h)), sspec),
        out_shape=(jax.ShapeDtypeStruct((DB, D), BF16), jax.ShapeDtypeStruct(s0.shape, s0.dtype)),
        compiler_params=_params(("parallel",)),
        name="hgrn_step",
    )(zcols, zrows, lb.reshape(D, 1), o_norm, s0)


def _router_kernel(h_ref, w_ref, b_ref, ri_ref, rg_ref, cnt_ref, carry_ref):
    @pl.when(pl.program_id(0) == 0)
    def _():
        carry_ref[...] = jnp.zeros_like(carry_ref)

    hb = h_ref[...].astype(BF16)
    tm = hb.shape[0]
    lane = lax.broadcasted_iota(I32, (tm, V7X_LANES), 1)
    lane_f = lane.astype(F32)
    valid = lane < N_GROUPS
    sc = [_sigmoid(jnp.dot(hb, w_ref[j], preferred_element_type=F32)) for j in range(EXPERTS_PER_GROUP)]
    sel = [jnp.where(valid, sc[j] + b_ref[j], NEG_INF) for j in range(EXPERTS_PER_GROUP)]
    gs = None
    for i in range(EXPERTS_PER_GROUP):
        for j in range(i + 1, EXPERTS_PER_GROUP):
            pair = sel[i] + sel[j]
            gs = pair if gs is None else jnp.maximum(gs, pair)
    gmax = jnp.max(gs, axis=-1, keepdims=True)
    gidx = jnp.min(jnp.where(gs == gmax, lane_f, float(V7X_LANES)), axis=-1, keepdims=True)
    in_g = lane_f == gidx
    pick = lambda x: jnp.sum(jnp.where(in_g, x, 0.0), axis=-1, keepdims=True)
    cand = [pick(sel[j]) for j in range(EXPERTS_PER_GROUP)]
    csc = [pick(sc[j]) for j in range(EXPERTS_PER_GROUP)]

    def first_argmax(vals):
        best, idx = vals[0], jnp.zeros_like(vals[0])
        for j in range(1, len(vals)):
            upd = vals[j] > best
            best = jnp.where(upd, vals[j], best)
            idx = jnp.where(upd, float(j), idx)
        return idx

    i1 = first_argmax(cand)
    i2 = first_argmax([jnp.where(i1 == float(j), NEG_INF, cand[j]) for j in range(EXPERTS_PER_GROUP)])
    at = lambda idx: sum(jnp.where(idx == float(j), csc[j], 0.0) for j in range(EXPERTS_PER_GROUP))
    w1, w2 = at(i1), at(i2)
    wsum = w1 + w2
    e1 = gidx * float(EXPERTS_PER_GROUP) + i1
    e2 = gidx * float(EXPERTS_PER_GROUP) + i2

    hit1 = lane_f == e1
    hit2 = lane_f == e2
    onehot = jnp.where(hit1 | hit2, 1.0, 0.0)
    lower = _iota01((tm, tm), lambda r, c: c < r)
    prior = jnp.dot(lower, onehot.astype(BF16), preferred_element_type=F32) + carry_ref[...]
    r1 = jnp.sum(jnp.where(hit1, prior, 0.0), axis=-1, keepdims=True)
    r2 = jnp.sum(jnp.where(hit2, prior, 0.0), axis=-1, keepdims=True)
    total = carry_ref[...] + jnp.sum(onehot, axis=0, keepdims=True)
    carry_ref[...] = total
    cnt_ref[...] = total.astype(I32)

    ints = jnp.where(lane == 0, e1, jnp.where(lane == 1, e2, jnp.where(lane == 2, r1, jnp.where(lane == 3, r2, 0.0))))
    ri_ref[...] = ints.astype(I32)
    rg_ref[...] = jnp.where(lane == 0, w1 / wsum, jnp.where(lane == 1, w2 / wsum, 0.0))


def moe_route(h, router_w, router_b, tm=256):
    T, D = h.shape
    tm = min(tm, T)
    pad = V7X_LANES - N_GROUPS
    wg = router_w.reshape(D, N_GROUPS, EXPERTS_PER_GROUP).transpose(2, 0, 1)
    wg = jnp.pad(wg, ((0, 0), (0, 0), (0, pad))).astype(BF16)
    bg = jnp.pad(router_b.astype(F32).reshape(N_GROUPS, EXPERTS_PER_GROUP).T, ((0, 0), (0, pad)))
    bg = bg.reshape(EXPERTS_PER_GROUP, 1, V7X_LANES)
    row = pl.BlockSpec((tm, V7X_LANES), lambda i: (i, 0))
    ri, rg, cnt = pl.pallas_call(
        _router_kernel,
        grid=(T // tm,),
        in_specs=[pl.BlockSpec((tm, D), lambda i: (i, 0)),
                  pl.BlockSpec((EXPERTS_PER_GROUP, D, V7X_LANES), lambda i: (0, 0, 0)),
                  pl.BlockSpec((EXPERTS_PER_GROUP, 1, V7X_LANES), lambda i: (0, 0, 0))],
        out_specs=(row, row, pl.BlockSpec((1, V7X_LANES), lambda i: (0, 0))),
        out_shape=(jax.ShapeDtypeStruct((T, V7X_LANES), I32), jax.ShapeDtypeStruct((T, V7X_LANES), F32),
                   jax.ShapeDtypeStruct((1, V7X_LANES), I32)),
        scratch_shapes=[pltpu.VMEM((1, V7X_LANES), F32)],
        compiler_params=_params(("arbitrary",)),
        name="moe_route",
    )(h, wg, bg)
    return ri[:, 0:2], ri[:, 2:4], rg, cnt[0, :N_EXPERTS]


def _dispatch_kernel(dest_ref, h_ref, init_ref, xb_ref, sem, *, tm):
    del init_ref
    base = pl.program_id(0) * tm

    def row_copy(r, k):
        d = dest_ref[(base + r) * TOP_K + k]
        return pltpu.make_async_copy(h_ref.at[pl.ds(r, 1)], xb_ref.at[pl.ds(d, 1)], sem)

    def start(r, _):
        for k in range(TOP_K):
            row_copy(r, k).start()
        return 0

    def wait(r, _):
        for k in range(TOP_K):
            row_copy(r, k).wait()
        return 0

    lax.fori_loop(0, tm, start, 0)
    lax.fori_loop(0, tm, wait, 0)


def moe_dispatch(h, dest, cap, tm=256):
    T, D = h.shape
    tm = min(tm, T)
    grid_spec = pltpu.PrefetchScalarGridSpec(
        num_scalar_prefetch=1,
        grid=(T // tm,),
        in_specs=[pl.BlockSpec((tm, D), lambda i, d: (i, 0)),
                  pl.BlockSpec(memory_space=pl.ANY)],
        out_specs=pl.BlockSpec(memory_space=pl.ANY),
        scratch_shapes=[pltpu.SemaphoreType.DMA(())],
    )
    return pl.pallas_call(
        functools.partial(_dispatch_kernel, tm=tm),
        grid_spec=grid_spec,
        out_shape=jax.ShapeDtypeStruct((cap, D), F32),
        input_output_aliases={2: 0},
        compiler_params=_params(("arbitrary",)),
        name="moe_dispatch",
    )(dest, h, jnp.zeros((cap, D), F32))


def _moe_up_kernel(sb_ref, sc_ref, ib_ref, ic_ref, ie_ref, first_ref, used_ref, ne_ref, nc_ref,
                   x_ref, wg_hbm, wu_hbm, a_ref, stage_ref, wbf_ref, sem, *, layer, ce):
    s = pl.program_id(0)

    def weight_copies(e, c):
        col = pl.multiple_of(c * ce, ce)
        return (pltpu.make_async_copy(wg_hbm.at[layer, e, :, pl.ds(col, ce)], stage_ref.at[0], sem.at[0]),
                pltpu.make_async_copy(wu_hbm.at[layer, e, :, pl.ds(col, ce)], stage_ref.at[1], sem.at[1]))

    @pl.when(s == 0)
    def _():
        for cp in weight_copies(ie_ref[0], ic_ref[0]):
            cp.start()

    @pl.when(used_ref[s] == 1)
    def _():
        @pl.when(first_ref[s] == 1)
        def _():
            for cp in weight_copies(ie_ref[s], ic_ref[s]):
                cp.wait()
            for w in range(2):
                for r in range(0, stage_ref.shape[1], CAST_ROWS):
                    wbf_ref[w, r:r + CAST_ROWS] = stage_ref[w, r:r + CAST_ROWS].astype(BF16)

            @pl.when(ne_ref[s] >= 0)
            def _():
                for cp in weight_copies(ne_ref[s], nc_ref[s]):
                    cp.start()

        xb = x_ref[...].astype(BF16)
        g = jnp.dot(xb, wbf_ref[0], preferred_element_type=F32)
        u = jnp.dot(xb, wbf_ref[1], preferred_element_type=F32)
        a_ref[...] = (_silu(g) * u).astype(BF16)

    @pl.when(used_ref[s] == 0)
    def _():
        a_ref[...] = jnp.zeros_like(a_ref)


def moe_up(xb, tables, w_gate, w_up, layer, blk, ce=512):
    cap, D = xb.shape
    DE = w_gate.shape[3]
    ce = min(ce, DE)
    nsteps = tables[0].shape[0]
    n_tab = len(tables)
    grid_spec = pltpu.PrefetchScalarGridSpec(
        num_scalar_prefetch=n_tab,
        grid=(nsteps,),
        in_specs=[pl.BlockSpec((blk, D), lambda s, *t: (t[2][s], 0)),
                  pl.BlockSpec(memory_space=pl.ANY), pl.BlockSpec(memory_space=pl.ANY)],
        out_specs=pl.BlockSpec((blk, ce), lambda s, *t: (t[0][s], t[1][s])),
        scratch_shapes=[pltpu.VMEM((2, D, ce), F32), pltpu.VMEM((2, D, ce), BF16),
                        pltpu.SemaphoreType.DMA((2,))],
    )
    return pl.pallas_call(
        functools.partial(_moe_up_kernel, layer=layer, ce=ce),
        grid_spec=grid_spec,
        out_shape=jax.ShapeDtypeStruct((cap, DE), BF16),
        compiler_params=_params(("arbitrary",)),
        name="moe_up",
    )(*tables, xb, w_gate, w_up)


def _moe_down_kernel(be_ref, ib_ref, first_ref, nu_ref, ne_ref,
                     a_ref, wd_hbm, y_ref, stage_ref, wdb_ref, sem, *, layer):
    i = pl.program_id(0)

    def weight_copy(e):
        return pltpu.make_async_copy(wd_hbm.at[layer, e], stage_ref, sem)

    @pl.when(i == 0)
    def _():
        weight_copy(be_ref[0]).start()

    @pl.when(i < nu_ref[0])
    def _():
        @pl.when(first_ref[i] == 1)
        def _():
            weight_copy(be_ref[i]).wait()
            for r in range(0, stage_ref.shape[0], CAST_ROWS):
                wdb_ref[r:r + CAST_ROWS] = stage_ref[r:r + CAST_ROWS].astype(BF16)

            @pl.when(ne_ref[i] >= 0)
            def _():
                weight_copy(ne_ref[i]).start()

        y_ref[...] = jnp.dot(a_ref[...], wdb_ref[...], preferred_element_type=F32)

    @pl.when(i >= nu_ref[0])
    def _():
        y_ref[...] = jnp.zeros_like(y_ref)


def moe_down(a, tables, w_down, layer, blk):
    cap, DE = a.shape
    D = w_down.shape[3]
    grid_spec = pltpu.PrefetchScalarGridSpec(
        num_scalar_prefetch=len(tables),
        grid=(cap // blk,),
        in_specs=[pl.BlockSpec((blk, DE), lambda i, *t: (t[1][i], 0)),
                  pl.BlockSpec(memory_space=pl.ANY)],
        out_specs=pl.BlockSpec((blk, D), lambda i, *t: (i, 0)),
        scratch_shapes=[pltpu.VMEM((DE, D), F32), pltpu.VMEM((DE, D), BF16), pltpu.SemaphoreType.DMA(())],
    )
    return pl.pallas_call(
        functools.partial(_moe_down_kernel, layer=layer),
        grid_spec=grid_spec,
        out_shape=jax.ShapeDtypeStruct((cap, D), F32),
        compiler_params=_params(("arbitrary",)),
        name="moe_down",
    )(*tables, a, w_down)


def _next_group(first, used, key_arrays):
    grp = jnp.cumsum(first * used) - 1
    n_groups = jnp.sum(first * used)
    nxt = jnp.minimum(grp + 1, n_groups - 1)
    has_next = (grp + 1 < n_groups) & (used == 1)
    outs = []
    for key in key_arrays:
        per_group = jnp.zeros_like(key).at[jnp.where(used == 1, grp, key.shape[0])].set(key, mode="drop")
        outs.append(jnp.where(has_next, per_group[nxt], -1).astype(I32))
    return outs


def _moe_tables(counts, blk, nblk, nch):
    nb = (counts + blk - 1) // blk
    bend = jnp.cumsum(nb)
    bstart = bend - nb
    nu = bend[-1]
    blocks = jnp.arange(nblk, dtype=I32)
    be = jnp.minimum(jnp.searchsorted(bend, blocks, side="right"), N_EXPERTS - 1).astype(I32)
    used_b = blocks < nu
    ib2 = jnp.minimum(blocks, nu - 1)
    be2 = be[ib2]
    first2 = ((blocks == 0) | (be2 != jnp.roll(be2, 1))).astype(I32)
    (ne2,) = _next_group(first2, used_b.astype(I32), (be2,))
    down = (be2, ib2, first2, nu.reshape(1).astype(I32), ne2)
    start_b = jnp.where(used_b, bstart[be], blocks)
    nb_b = jnp.where(used_b, nb[be], 1)
    chunks = jnp.arange(nch, dtype=I32)
    pos = start_b[:, None] * nch + chunks[None, :] * nb_b[:, None] + (blocks - start_b)[:, None]
    nsteps = nblk * nch
    sb = jnp.zeros((nsteps,), I32).at[pos.reshape(-1)].set(jnp.repeat(blocks, nch))
    sc = jnp.zeros((nsteps,), I32).at[pos.reshape(-1)].set(jnp.tile(chunks, nblk))
    steps = jnp.arange(nsteps, dtype=I32)
    used = (steps < nu * nch).astype(I32)
    clamp = jnp.minimum(steps, nu * nch - 1)
    ib, ic = sb[clamp], sc[clamp]
    ie = be[ib]
    first = ((steps == 0) | (ie != jnp.roll(ie, 1)) | (ic != jnp.roll(ic, 1))).astype(I32)
    ne, nc = _next_group(first, used, (ie, ic))
    up = (sb, sc, ib, ic, ie, first, used, ne, nc)
    return bstart * blk, up, down


def _combine_kernel(dest_ref, yb_ref, x_ref, gate_ref, rg_ref, o_ref, ybuf_ref, sem, *, tm):
    base = pl.program_id(0) * tm

    def row_copy(r, k):
        d = dest_ref[(base + r) * TOP_K + k]
        return pltpu.make_async_copy(yb_ref.at[pl.ds(d, 1)], ybuf_ref.at[k, pl.ds(r, 1)], sem)

    def start(r, _):
        for k in range(TOP_K):
            row_copy(r, k).start()
        return 0

    def wait(r, _):
        for k in range(TOP_K):
            row_copy(r, k).wait()
        return 0

    lax.fori_loop(0, tm, start, 0)
    lax.fori_loop(0, tm, wait, 0)
    rg = rg_ref[...]
    y = ybuf_ref[0] * rg[:, 0:1] + ybuf_ref[1] * rg[:, 1:2]
    o_ref[...] = x_ref[...] + gate_ref[...] * y


def moe_combine(yb, dest, x, gate, rg, rows_per_batch, tm=256):
    T, D = x.shape
    tm = min(tm, T)
    if gate.shape[1] == 1:
        rpt = rows_per_batch // tm
        gspec = pl.BlockSpec((None, 1, D), lambda i, d: (i // rpt, 0, 0))
    else:
        gspec = pl.BlockSpec((None, tm, D), lambda i, d: (0, i, 0))
    grid_spec = pltpu.PrefetchScalarGridSpec(
        num_scalar_prefetch=1,
        grid=(T // tm,),
        in_specs=[pl.BlockSpec(memory_space=pl.ANY),
                  pl.BlockSpec((tm, D), lambda i, d: (i, 0)),
                  gspec,
                  pl.BlockSpec((tm, V7X_LANES), lambda i, d: (i, 0))],
        out_specs=pl.BlockSpec((tm, D), lambda i, d: (i, 0)),
        scratch_shapes=[pltpu.VMEM((TOP_K, tm, D), F32), pltpu.SemaphoreType.DMA(())],
    )
    return pl.pallas_call(
        functools.partial(_combine_kernel, tm=tm),
        grid_spec=grid_spec,
        out_shape=jax.ShapeDtypeStruct((T, D), F32),
        compiler_params=_params(("arbitrary",)),
        name="moe_combine",
    )(dest, yb, x, gate, rg)


def moe_sublayer(h, x, gate, rows_per_batch, router_w, router_b, w_gate, w_up, w_down, layer, blk, ce=512):
    T, D = h.shape
    DE = w_gate.shape[3]
    ce = min(ce, DE)
    e_idx, rank, rg, counts = moe_route(h, router_w, router_b, tm=min(256, T))
    A = T * TOP_K
    cap = -(-(A + N_EXPERTS * (blk - 1)) // blk) * blk
    pad_start, up_tables, down_tables = _moe_tables(counts, blk, cap // blk, DE // ce)
    dest = (pad_start[e_idx] + rank).reshape(-1).astype(I32)
    xb = moe_dispatch(h, dest, cap, tm=min(256, T))
    a = moe_up(xb, up_tables, w_gate, w_up, layer, blk, ce)
    yb = moe_down(a, down_tables, w_down, layer, blk)
    return moe_combine(yb, dest, x, gate, rg, rows_per_batch, tm=min(256, T))


def kernel(x_prompt, x_sample, c_prompt, c_sample, cache_k, cache_v, cache_logf, state_hgrn, page_table,
           ada_w, ada_b, norm_g, att_w_in, att_b_f, att_q_norm, att_k_norm, att_w_o,
           rec_w_in, rec_lb_logits, rec_o_norm, rec_w_o, router_w, router_b,
           moe_w_gate, moe_w_up, moe_w_down):
    B, L, D = x_prompt.shape
    DB, LS, _ = x_sample.shape
    depth = ada_w.shape[0]
    H = ATT_HEADS
    hd = D // H
    T = B * L
    TS = DB * LS
    assert LS == 1
    P = cache_k.shape[2]

    p_lb = jax.nn.softmax(rec_lb_logits.astype(F32), axis=0)
    lb_table = jnp.cumsum(p_lb, axis=0) - p_lb[0]

    n_c = B + DB
    r_c = -(-n_c // V7X_SUBLANES) * V7X_SUBLANES
    c_all = jnp.pad(jnp.concatenate([c_prompt, c_sample], axis=0), ((0, r_c - n_c), (0, 0)))
    mod = ada_modulation(c_all, ada_w.reshape(depth * 2, D, 3 * D), ada_b.reshape(depth * 2, 1, 3 * D))
    mod = mod.reshape(depth, 2, r_c, 3, D)

    def mods(layer, sub):
        m = mod[layer, sub]
        shift, scale, gate = m[:, 0], m[:, 1], m[:, 2]
        pr = lambda a: a[:B].reshape(B, 1, D)
        sa = lambda a: a[B:n_c].reshape(1, DB, D)
        return (pr(shift), pr(scale), pr(gate)), (sa(shift), sa(scale), sa(gate))

    xp = x_prompt.reshape(T, D)
    xs = x_sample.reshape(TS, D)
    outs = {n: [] for n in ("kp", "vp", "lfp", "ks", "vs", "lfs", "sp", "ss")}
    blk_p = max(16, min(256, T * TOP_K // N_EXPERTS))
    blk_s = max(16, min(256, TS * TOP_K // N_EXPERTS))
    n_pool = cache_k.shape[1]
    ck = cache_k.reshape(cache_k.shape[0], n_pool, P * H, hd)
    cv = cache_v.reshape(cache_v.shape[0], n_pool, P * H, hd)

    for layer in range(depth):
        j = layer // 2
        (shp, scp, gp), (shs, scs, gs) = mods(layer, 0)
        g0 = norm_g[layer, 0].reshape(1, D)
        hp = norm_modulate(xp.reshape(B, L, D), g0, scp, shp).reshape(T, D)
        hs = norm_modulate(xs.reshape(1, TS, D), g0, scs, shs).reshape(TS, D)
        if layer % 2 == 0:
            w_in = att_w_in
            w_f = att_w_in[j][:, 3 * D:].reshape(1, D, H)
            b_f = att_b_f[j].reshape(1, H)
            qg = att_q_norm[j].reshape(1, hd)
            kg = att_k_norm[j].reshape(1, hd)
            q_bf = project(hp, w_in, layer=j, mode="headnorm_bf", col_off=0, n_out=D, extra=(qg,))
            k_f, k_bf = project(hp, w_in, layer=j, mode="headnorm", col_off=D, n_out=D, extra=(kg,))
            v_f, v_bf = project(hp, w_in, layer=j, mode="dual", col_off=2 * D, n_out=D)
            lf = project(hp, w_f, mode="logsig", extra=(b_f,)).reshape(B, L, H)
            cum = seq_cumsum(lf)
            cum_t = jnp.transpose(cum, (0, 2, 1)).reshape(B, H, 1, L)
            op = fox_prompt_attention(q_bf.reshape(B, L, D), k_bf.reshape(B, L, D), v_bf.reshape(B, L, D),
                                      cum, cum_t, H)
            xp = project(op.reshape(T, D), att_w_o, layer=j, mode="resid", extra=(xp, gp), rows_per_batch=L)
            outs["kp"].append(k_f.reshape(B, L, H, hd))
            outs["vp"].append(v_f.reshape(B, L, H, hd))
            outs["lfp"].append(lf)
            qs, _ = project(hs, w_in, layer=j, mode="headnorm", col_off=0, n_out=D, extra=(qg,))
            ks, _ = project(hs, w_in, layer=j, mode="headnorm", col_off=D, n_out=D, extra=(kg,))
            vs = project(hs, w_in, layer=j, mode="plain", col_off=2 * D, n_out=D)
            lfs = project(hs, w_f, mode="logsig", extra=(b_f,))
            bias = fox_sample_bias(page_table, cache_logf, j, lfs.reshape(DB, 1, H))
            osm = fox_sample_attention(page_table, qs.reshape(DB, H, hd), ks.reshape(DB, H, hd),
                                       vs.reshape(DB, H, hd), ck, cv, j,
                                       bias.reshape(DB, bias.shape[1], 1, P * H))
            xs = project(osm.reshape(TS, D), att_w_o, layer=j, mode="resid", extra=(xs, gs))
            outs["ks"].append(ks.reshape(DB, LS, H, hd))
            outs["vs"].append(vs.reshape(DB, LS, H, hd))
            outs["lfs"].append(lfs.reshape(DB, LS, H))
        else:
            lb = lb_table[layer].reshape(1, D)
            on = rec_o_norm[j].reshape(1, hd)
            zp = project(hp, rec_w_in, layer=j, mode="plain")
            s0p = jnp.zeros((B, REC_HEADS, hd, hd), state_hgrn.dtype)
            op, sp = hgrn_prompt(zp.reshape(B, L, 4 * D), lb, on, s0p, REC_HEADS)
            xp = project(op.reshape(T, D), rec_w_o, layer=j, mode="resid", extra=(xp, gp), rows_per_batch=L)
            zs = project(hs, rec_w_in, layer=j, mode="plain")
            osm, ss = hgrn_step(zs, lb_table[layer], on, state_hgrn[j], REC_HEADS)
            xs = project(osm, rec_w_o, layer=j, mode="resid", extra=(xs, gs))
            outs["sp"].append(sp)
            outs["ss"].append(ss)

        (shp, scp, gp), (shs, scs, gs) = mods(layer, 1)
        g1 = norm_g[layer, 1].reshape(1, D)
        hp = norm_modulate(xp.reshape(B, L, D), g1, scp, shp, out_dtype=F32).reshape(T, D)
        hs = norm_modulate(xs.reshape(1, TS, D), g1, scs, shs, out_dtype=F32).reshape(TS, D)
        moe_w = (router_w, router_b, moe_w_gate, moe_w_up, moe_w_down, layer)
        xp = moe_sublayer(hp, xp, gp, L, *moe_w, blk=blk_p)
        xs = moe_sublayer(hs, xs, gs, None, *moe_w, blk=blk_s)

    st = lambda n: jnp.stack(outs[n])
    return (xp.reshape(B, L, D), xs.reshape(DB, LS, D), st("kp"), st("vp"), st("lfp"), st("ks"), st("vs"),
            st("lfs"), st("sp"), st("ss"))
```

```python
import functools

import jax
import jax.numpy as jnp
from jax import lax
from jax.experimental import pallas as pl
from jax.experimental.pallas import tpu as pltpu

F32 = jnp.float32
BF16 = jnp.bfloat16
I32 = jnp.int32
EPS = 1e-6

V7X_LANES = 128
V7X_SUBLANES = 8
V7X_VMEM_LIMIT_BYTES = 56 * 1024 * 1024

ATT_HEADS = 32
REC_HEADS = 32
N_GROUPS = 8
EXPERTS_PER_GROUP = 4
N_EXPERTS = N_GROUPS * EXPERTS_PER_GROUP
TOP_K = 2
REC_SUB = 16
NEG_INF = float("-inf")
LOG2E = 1.4426950408889634
CAST_ROWS = 256


def _params(sem):
    return pltpu.CompilerParams(dimension_semantics=sem, vmem_limit_bytes=V7X_VMEM_LIMIT_BYTES)


def _split3(x):
    x1 = x.astype(BF16)
    r1 = x - x1.astype(F32)
    x2 = r1.astype(BF16)
    x3 = (r1 - x2.astype(F32)).astype(BF16)
    return x1, x2, x3


def _dot01(m01, x):
    x1, x2, x3 = _split3(x)
    d = lambda b: jnp.dot(m01, b, preferred_element_type=F32)
    return d(x1) + (d(x2) + d(x3))


def _iota01(shape, pred):
    r = lax.broadcasted_iota(I32, shape, 0)
    c = lax.broadcasted_iota(I32, shape, 1)
    return jnp.where(pred(r, c), 1.0, 0.0).astype(BF16)


def _sigmoid(x):
    return 1.0 / (1.0 + jnp.exp(-x))


def _silu(x):
    return x * _sigmoid(x)


def _log_sigmoid(x):
    return jnp.minimum(x, 0.0) - jnp.log1p(jnp.exp(-jnp.abs(x)))


def _logaddexp(a, b):
    return jnp.maximum(a, b) + jnp.log1p(jnp.exp(-jnp.abs(a - b)))


def _ada_kernel(c_ref, w_ref, b_ref, o_ref):
    sc = _silu(c_ref[...]).astype(BF16)
    o_ref[...] = jnp.dot(sc, w_ref[...].astype(BF16), preferred_element_type=F32) + b_ref[...]


def ada_modulation(c_all, ada_w, ada_b, tn=512):
    S, D, N = ada_w.shape
    R = c_all.shape[0]
    tn = min(tn, N)
    return pl.pallas_call(
        _ada_kernel,
        grid=(S, N // tn),
        in_specs=[pl.BlockSpec((R, D), lambda s, j: (0, 0)),
                  pl.BlockSpec((None, D, tn), lambda s, j: (s, 0, j)),
                  pl.BlockSpec((None, 1, tn), lambda s, j: (s, 0, j))],
        out_specs=pl.BlockSpec((None, R, tn), lambda s, j: (s, 0, j)),
        out_shape=jax.ShapeDtypeStruct((S, R, N), F32),
        compiler_params=_params(("parallel", "parallel")),
        name="ada_modulation",
    )(c_all, ada_w, ada_b)


def _norm_mod_kernel(x_ref, g_ref, scale_ref, shift_ref, o_ref):
    x = x_ref[...]
    y = x * lax.rsqrt(jnp.mean(x * x, axis=-1, keepdims=True) + EPS)
    h = (y * g_ref[...]) * (1.0 + scale_ref[...]) + shift_ref[...]
    o_ref[...] = h.astype(o_ref.dtype)


def norm_modulate(x, g, scale, shift, out_dtype=BF16, tr=256):
    NB, L, D = x.shape
    LS = scale.shape[1]
    tr = min(tr, L)
    ts = tr if LS == L else 1
    smap = (lambda b, i: (b, i, 0)) if LS == L else (lambda b, i: (b, 0, 0))
    return pl.pallas_call(
        _norm_mod_kernel,
        grid=(NB, L // tr),
        in_specs=[pl.BlockSpec((None, tr, D), lambda b, i: (b, i, 0)),
                  pl.BlockSpec((1, D), lambda b, i: (0, 0)),
                  pl.BlockSpec((None, ts, D), smap),
                  pl.BlockSpec((None, ts, D), smap)],
        out_specs=pl.BlockSpec((None, tr, D), lambda b, i: (b, i, 0)),
        out_shape=jax.ShapeDtypeStruct((NB, L, D), out_dtype),
        compiler_params=_params(("parallel", "parallel")),
        name="norm_modulate",
    )(x, g, scale, shift)


def _head_rms(acc, g):
    parts = []
    for c in range(acc.shape[1] // V7X_LANES):
        blk = acc[:, c * V7X_LANES:(c + 1) * V7X_LANES]
        y = blk * lax.rsqrt(jnp.mean(blk * blk, axis=-1, keepdims=True) + EPS)
        parts.append(y * g)
    return jnp.concatenate(parts, axis=-1) if len(parts) > 1 else parts[0]


def _proj_kernel(*refs, mode):
    a_ref, w_ref = refs[0], refs[1]
    wbf_ref = refs[-1]

    @pl.when(pl.program_id(1) == 0)
    def _():
        wbf_ref[...] = w_ref[...].astype(BF16)

    acc = jnp.dot(a_ref[...], wbf_ref[...], preferred_element_type=F32)
    if mode == "plain":
        refs[2][...] = acc
    elif mode == "headnorm":
        y = _head_rms(acc, refs[2][...])
        refs[3][...] = y
        refs[4][...] = y.astype(BF16)
    elif mode == "headnorm_bf":
        refs[3][...] = _head_rms(acc, refs[2][...]).astype(BF16)
    elif mode == "dual":
        refs[2][...] = acc
        refs[3][...] = acc.astype(BF16)
    elif mode == "logsig":
        refs[3][...] = _log_sigmoid(acc + refs[2][...])
    elif mode == "resid":
        refs[4][...] = refs[2][...] + refs[3][...] * acc
    else:
        raise ValueError(mode)


def project(a, w, *, mode, layer=0, col_off=0, n_out=None, extra=(), rows_per_batch=None, tm=1024, tn=512):
    M, K = a.shape
    n_out = w.shape[2] - col_off if n_out is None else n_out
    tm = min(tm, M)
    if mode == "resid" and extra[1].shape[1] == 1:
        tm = min(tm, rows_per_batch)
    tn = min(tn, n_out)
    assert M % tm == 0 and n_out % tn == 0 and col_off % tn == 0
    joff = col_off // tn
    grid = (n_out // tn, M // tm)
    in_specs = [pl.BlockSpec((tm, K), lambda j, i: (i, 0)),
                pl.BlockSpec((None, K, tn), lambda j, i: (layer, 0, j + joff))]
    o_spec = pl.BlockSpec((tm, tn), lambda j, i: (i, j))
    f32_out = jax.ShapeDtypeStruct((M, n_out), F32)
    bf_out = jax.ShapeDtypeStruct((M, n_out), BF16)
    if mode == "plain":
        out_specs, out_shape = o_spec, f32_out
    elif mode == "headnorm":
        in_specs.append(pl.BlockSpec((1, V7X_LANES), lambda j, i: (0, 0)))
        out_specs, out_shape = (o_spec, o_spec), (f32_out, bf_out)
    elif mode == "headnorm_bf":
        in_specs.append(pl.BlockSpec((1, V7X_LANES), lambda j, i: (0, 0)))
        out_specs, out_shape = o_spec, bf_out
    elif mode == "dual":
        out_specs, out_shape = (o_spec, o_spec), (f32_out, bf_out)
    elif mode == "logsig":
        in_specs.append(pl.BlockSpec((1, tn), lambda j, i: (0, j)))
        out_specs, out_shape = o_spec, f32_out
    elif mode == "resid":
        in_specs.append(o_spec)
        if extra[1].shape[1] == 1:
            assert rows_per_batch % tm == 0
            rpt = rows_per_batch // tm
            in_specs.append(pl.BlockSpec((None, 1, tn), lambda j, i: (i // rpt, 0, j)))
        else:
            assert extra[1].shape[:2] == (1, M)
            in_specs.append(pl.BlockSpec((None, tm, tn), lambda j, i: (0, i, j)))
        out_specs, out_shape = o_spec, f32_out
    else:
        raise ValueError(mode)
    return pl.pallas_call(
        functools.partial(_proj_kernel, mode=mode),
        grid=grid,
        in_specs=in_specs,
        out_specs=out_specs,
        out_shape=out_shape,
        scratch_shapes=[pltpu.VMEM((K, tn), BF16)],
        compiler_params=_params(("parallel", "arbitrary")),
        name="project_" + mode,
    )(a, w, *extra)


def _cumsum_kernel(x_ref, o_ref, carry_ref):
    @pl.when(pl.program_id(1) == 0)
    def _():
        carry_ref[...] = jnp.zeros_like(carry_ref)

    x = x_ref[...]
    n = x.shape[0]
    tril = _iota01((n, n), lambda r, c: c <= r)
    cs = _dot01(tril, x) + carry_ref[...]
    o_ref[...] = cs
    carry_ref[...] = cs[n - 1:n, :]


def seq_cumsum(x, tc=256):
    B, L, H = x.shape
    tc = min(tc, L)
    return pl.pallas_call(
        _cumsum_kernel,
        grid=(B, L // tc),
        in_specs=[pl.BlockSpec((None, tc, H), lambda b, i: (b, i, 0))],
        out_specs=pl.BlockSpec((None, tc, H), lambda b, i: (b, i, 0)),
        out_shape=jax.ShapeDtypeStruct((B, L, H), F32),
        scratch_shapes=[pltpu.VMEM((1, H), F32)],
        compiler_params=_params(("parallel", "arbitrary")),
        name="seq_cumsum",
    )(x)


def _fox_prompt_kernel(q_ref, k_ref, v_ref, cq_ref, ck_ref, o_ref, *, tq, tk, scale):
    i = pl.program_id(2)
    h = pl.program_id(1)
    q = q_ref[...]
    cq_all = cq_ref[...]
    lane = lax.broadcasted_iota(I32, cq_all.shape, 1)
    cq = jnp.sum(jnp.where(lane == h, cq_all, 0.0), axis=-1, keepdims=True)
    hd = q.shape[-1]

    def scores(j, masked):
        ks = pl.multiple_of(j * tk, tk)
        s = lax.dot_general(q, k_ref[pl.ds(ks, tk), :], (((1,), (1,)), ((), ())),
                            preferred_element_type=F32) * scale
        s = s + (cq - ck_ref[:, pl.ds(ks, tk)])
        if masked:
            qpos = i * tq + lax.broadcasted_iota(I32, (tq, tk), 0)
            kpos = j * tk + lax.broadcasted_iota(I32, (tq, tk), 1)
            s = jnp.where(kpos <= qpos, s, NEG_INF)
        return s

    def stats(j, carry, masked):
        m, l = carry
        s = scores(j, masked)
        m_new = jnp.maximum(m, jnp.max(s, axis=-1, keepdims=True))
        l = l * jnp.exp(m - m_new) + jnp.sum(jnp.exp(s - m_new), axis=-1, keepdims=True)
        return m_new, l

    nfull = (i * tq) // tk
    diag = range(tq // tk)
    carry = lax.fori_loop(0, nfull, lambda j, c: stats(j, c, False),
                          (jnp.full((tq, 1), NEG_INF, F32), jnp.zeros((tq, 1), F32)))
    for d in diag:
        carry = stats(nfull + d, carry, True)
    m, l = carry
    inv_l = 1.0 / l

    def weighted(j, acc, masked):
        p = jnp.exp(scores(j, masked) - m) * inv_l
        ks = pl.multiple_of(j * tk, tk)
        return acc + jnp.dot(p.astype(BF16), v_ref[pl.ds(ks, tk), :], preferred_element_type=F32)

    acc = lax.fori_loop(0, nfull, lambda j, c: weighted(j, c, False), jnp.zeros((tq, hd), F32))
    for d in diag:
        acc = weighted(nfull + d, acc, True)
    o_ref[...] = acc.astype(o_ref.dtype)


def fox_prompt_attention(q, k, v, cum, cum_t, n_heads, tq=512, tk=512):
    B, L, D = q.shape
    hd = D // n_heads
    tq = min(tq, L)
    tk = min(tk, tq)
    assert L % tq == 0 and tq % tk == 0
    kern = functools.partial(_fox_prompt_kernel, tq=tq, tk=tk, scale=hd ** -0.5)
    return pl.pallas_call(
        kern,
        grid=(B, n_heads, L // tq),
        in_specs=[pl.BlockSpec((None, tq, hd), lambda b, h, i: (b, i, h)),
                  pl.BlockSpec((None, L, hd), lambda b, h, i: (b, 0, h)),
                  pl.BlockSpec((None, L, hd), lambda b, h, i: (b, 0, h)),
                  pl.BlockSpec((None, tq, n_heads), lambda b, h, i: (b, i, 0)),
                  pl.BlockSpec((None, None, 1, L), lambda b, h, i: (b, h, 0, 0))],
        out_specs=pl.BlockSpec((None, tq, hd), lambda b, h, i: (b, i, h)),
        out_shape=jax.ShapeDtypeStruct((B, L, D), BF16),
        compiler_params=_params(("parallel", "parallel", "arbitrary")),
        name="fox_prompt_attention",
    )(q, k, v, cum, cum_t)


def _fox_tail_kernel(pt_ref, *refs, n_seq):
    lf_refs, (cn_ref, o_ref, carry_ref) = refs[:n_seq], refs[n_seq:]

    @pl.when(pl.program_id(0) == 0)
    def _():
        carry_ref[...] = jnp.zeros_like(carry_ref)

    n = lf_refs[0].shape[0]
    upper = _iota01((n, n), lambda r, c: c > r)
    for b in range(n_seq):
        lf = lf_refs[b][...]
        tl = _dot01(upper, lf) + carry_ref[b]
        o_ref[b] = tl + cn_ref[b]
        carry_ref[b] = tl[0:1, :] + lf[0:1, :]


def fox_sample_bias(page_table, cache_logf, layer, cn):
    DB, NP = page_table.shape
    _, _, P, H = cache_logf.shape
    page = lambda b: pl.BlockSpec((None, None, P, H), lambda p, pt: (layer, pt[b * NP + NP - 1 - p], 0, 0))
    grid_spec = pltpu.PrefetchScalarGridSpec(
        num_scalar_prefetch=1,
        grid=(NP,),
        in_specs=[page(b) for b in range(DB)] + [pl.BlockSpec((DB, 1, H), lambda p, pt: (0, 0, 0))],
        out_specs=pl.BlockSpec((DB, None, P, H), lambda p, pt: (0, NP - 1 - p, 0, 0)),
        scratch_shapes=[pltpu.VMEM((DB, 1, H), F32)],
    )
    return pl.pallas_call(
        functools.partial(_fox_tail_kernel, n_seq=DB),
        grid_spec=grid_spec,
        out_shape=jax.ShapeDtypeStruct((DB, NP, P, H), F32),
        compiler_params=_params(("arbitrary",)),
        name="fox_sample_bias",
    )(page_table.reshape(-1), *([cache_logf] * DB), cn)


def _fox_sample_kernel(pt_ref, q_ref, kn_ref, vn_ref, *refs, scale, n_heads, pps):
    ck_refs, cv_refs = refs[:pps], refs[pps:2 * pps]
    bias_ref, o_ref, m_ref, l_ref, acc_ref, mask_ref = refs[2 * pps:]
    p = pl.program_id(1)

    @pl.when(p == 0)
    def _():
        m_ref[...] = jnp.full_like(m_ref, NEG_INF)
        l_ref[...] = jnp.zeros_like(l_ref)
        acc_ref[...] = jnp.zeros_like(acc_ref)
        row = lax.broadcasted_iota(I32, mask_ref.shape, 0)
        col = lax.broadcasted_iota(I32, mask_ref.shape, 1)
        mask_ref[...] = jnp.where((col & (n_heads - 1)) == row, 0.0, NEG_INF)

    q = q_ref[...]
    qb = q.astype(BF16)
    m_new, l, acc = m_ref[...], l_ref[...], acc_ref[...]
    for i in range(pps):
        kb = ck_refs[i][...].astype(BF16)
        s = lax.dot_general(qb, kb, (((1,), (1,)), ((), ())), preferred_element_type=F32) * scale
        s = s + bias_ref[i] + mask_ref[...]
        m = m_new
        m_new = jnp.maximum(m, jnp.max(s, axis=-1, keepdims=True))
        a = jnp.exp(m - m_new)
        pe = jnp.exp(s - m_new)
        l = l * a + jnp.sum(pe, axis=-1, keepdims=True)
        acc = acc * a + jnp.dot(pe.astype(BF16), cv_refs[i][...].astype(BF16), preferred_element_type=F32)
    m_ref[...] = m_new
    l_ref[...] = l
    acc_ref[...] = acc

    @pl.when(p == pl.num_programs(1) - 1)
    def _():
        s_self = jnp.sum(q * kn_ref[...], axis=-1, keepdims=True) * scale
        m_fin = jnp.maximum(m_new, s_self)
        a_fin = jnp.exp(m_new - m_fin)
        p_self = jnp.exp(s_self - m_fin)
        l_fin = l * a_fin + p_self
        acc_fin = acc * a_fin + p_self * vn_ref[...]
        o_ref[...] = (acc_fin / l_fin).astype(o_ref.dtype)


def fox_sample_attention(page_table, q, k_new, v_new, cache_k, cache_v, layer, bias, pps=4):
    DB, NP = page_table.shape
    _, H, hd = q.shape
    PH = cache_k.shape[2]
    pps = min(pps, NP)
    assert H & (H - 1) == 0 and NP % pps == 0
    tok = pl.BlockSpec((None, H, hd), lambda b, p, pt: (b, 0, 0))
    page = lambda i: pl.BlockSpec((None, None, PH, hd), lambda b, p, pt: (layer, pt[b * NP + p * pps + i], 0, 0))
    pages = [page(i) for i in range(pps)]
    grid_spec = pltpu.PrefetchScalarGridSpec(
        num_scalar_prefetch=1,
        grid=(DB, NP // pps),
        in_specs=[tok, tok, tok] + pages + pages +
                 [pl.BlockSpec((None, pps, 1, PH), lambda b, p, pt: (b, p, 0, 0))],
        out_specs=tok,
        scratch_shapes=[pltpu.VMEM((H, 1), F32), pltpu.VMEM((H, 1), F32), pltpu.VMEM((H, hd), F32),
                        pltpu.VMEM((H, PH), F32)],
    )
    return pl.pallas_call(
        functools.partial(_fox_sample_kernel, scale=hd ** -0.5, n_heads=H, pps=pps),
        grid_spec=grid_spec,
        out_shape=jax.ShapeDtypeStruct((DB, H, hd), BF16),
        compiler_params=_params(("parallel", "arbitrary")),
        name="fox_sample_attention",
    )(page_table.reshape(-1), q, k_new, v_new, *([cache_k] * pps), *([cache_v] * pps), bias)


def _hgrn_gates(zq, zf, lb):
    qt = _silu(zq)
    g = _logaddexp(jnp.log(lb), jnp.log1p(-lb) + _log_sigmoid(zf))
    kk = (1.0 - lb) * _sigmoid(-zf)
    return qt, kk, g


def _hgrn_prompt_kernel(zq_ref, zf_ref, zi_ref, zg_ref, lb_ref, on_ref, s0_ref, o_ref, sout_ref,
                        st_ref, k_scr, b2_scr, kv_scr, oi_scr, *, tb, hp):
    hd = on_ref.shape[-1]
    for j in range(hp):
        lanes = slice(j * hd, (j + 1) * hd)
        _hgrn_prompt_head(zq_ref.at[:, lanes], zf_ref.at[:, lanes], zi_ref.at[:, lanes], zg_ref.at[:, lanes],
                          lb_ref.at[:, lanes], on_ref, s0_ref.at[j], o_ref.at[:, lanes], sout_ref.at[j],
                          st_ref.at[j], k_scr.at[j], b2_scr.at[j], kv_scr.at[j], oi_scr.at[j], tb=tb)


def _hgrn_prompt_head(zq_ref, zf_ref, zi_ref, zg_ref, lb_ref, on_ref, s0_ref, o_ref, sout_ref,
                      st_ref, k_scr, b2_scr, kv_scr, oi_scr, *, tb):
    t = pl.program_id(2)
    C = REC_SUB
    nc = tb // C
    half = V7X_SUBLANES

    @pl.when(t == 0)
    def _():
        st_ref[...] = s0_ref[...].T

    qt, kk, g = _hgrn_gates(zq_ref[...], zf_ref[...], lb_ref[...])
    kdim = qt.shape[-1]
    sh = C.bit_length() - 1
    b = _dot01(_iota01((tb, tb), lambda r, c: ((r >> sh) == (c >> sh)) & (c <= r)), g)
    b2 = b * LOG2E
    k_scr[...] = kk
    b2_scr[...] = b2
    vt = zi_ref[...].T.astype(BF16)

    for c in range(nc):
        rows = slice(c * C, (c + 1) * C)
        btot = b[c * C + C - 1:(c + 1) * C, :]
        ke = (kk[rows, :] * jnp.exp(btot - b[rows, :])).astype(BF16)
        kv_scr[c] = jnp.dot(vt[:, rows], ke, preferred_element_type=F32)

    st = st_ref[...]
    for c in range(nc):
        rows = slice(c * C, (c + 1) * C)
        qe = (qt[rows, :] * jnp.exp(b[rows, :])).astype(BF16)
        oi_scr[rows, :] = lax.dot_general(qe, st.astype(BF16), (((1,), (1,)), ((), ())),
                                          preferred_element_type=F32)
        st = st * jnp.exp(b[c * C + C - 1:(c + 1) * C, :]) + kv_scr[c]
    st_ref[...] = st

    o_parts = []
    for c in range(nc):
        r0 = c * C
        for hh in range(C // half):
            h0 = r0 + hh * half
            qh = qt[h0:h0 + half, :]
            bh = b2[h0:h0 + half, :]
            tau = hh * half + lax.broadcasted_iota(I32, (half, kdim), 0)
            acc = jnp.zeros((half, kdim), F32)
            for sig in range(min(C, (hh + 1) * half)):
                d = bh - b2_scr[pl.ds(r0 + sig, 1), :]
                if sig > hh * half:
                    d = jnp.where(tau >= sig, d, NEG_INF)
                a = jnp.sum(qh * k_scr[pl.ds(r0 + sig, 1), :] * jnp.exp2(d), axis=-1, keepdims=True)
                acc = acc + a * zi_ref[pl.ds(r0 + sig, 1), :]
            o_parts.append(acc)

    o = jnp.concatenate(o_parts, axis=0) + oi_scr[...]
    y = o * lax.rsqrt(jnp.mean(o * o, axis=-1, keepdims=True) + EPS) * on_ref[...]
    o_ref[...] = (y * _silu(zg_ref[...])).astype(o_ref.dtype)

    @pl.when(t == pl.num_programs(2) - 1)
    def _():
        sout_ref[...] = st.T


def hgrn_prompt(z, lb, o_norm, s0, n_heads, tb=256, hp=4):
    B, L, D4 = z.shape
    D = D4 // 4
    hd = D // n_heads
    tb = min(tb, L)
    hp = min(hp, n_heads)
    assert L % tb == 0 and tb % REC_SUB == 0 and n_heads % hp == 0
    ng = n_heads // hp
    zspec = lambda part: pl.BlockSpec((None, tb, hp * hd), lambda b, h, t: (b, t, part * ng + h))
    sspec = pl.BlockSpec((None, hp, hd, hd), lambda b, h, t: (b, h, 0, 0))
    return pl.pallas_call(
        functools.partial(_hgrn_prompt_kernel, tb=tb, hp=hp),
        grid=(B, ng, L // tb),
        in_specs=[zspec(0), zspec(1), zspec(2), zspec(3),
                  pl.BlockSpec((1, hp * hd), lambda b, h, t: (0, h)),
                  pl.BlockSpec((1, hd), lambda b, h, t: (0, 0)),
                  sspec],
        out_specs=(pl.BlockSpec((None, tb, hp * hd), lambda b, h, t: (b, t, h)), sspec),
        out_shape=(jax.ShapeDtypeStruct((B, L, D), BF16), jax.ShapeDtypeStruct(s0.shape, s0.dtype)),
        scratch_shapes=[pltpu.VMEM((hp, hd, hd), F32), pltpu.VMEM((hp, tb, hd), F32), pltpu.VMEM((hp, tb, hd), F32),
                        pltpu.VMEM((hp, tb // REC_SUB, hd, hd), F32), pltpu.VMEM((hp, tb, hd), F32)],
        compiler_params=_params(("parallel", "parallel", "arbitrary")),
        name="hgrn_prompt",
    )(z, z, z, z, lb, o_norm, s0)


def _hgrn_step_kernel(zc_ref, zg_ref, lbc_ref, on_ref, s0_ref, o_ref, sout_ref, *, n_seq):
    lbc = lbc_ref[...]
    rows = []
    for b in range(n_seq):
        col = lambda part: zc_ref[part, :, b:b + 1]
        qt, kk, g = _hgrn_gates(col(0), col(1), lbc)
        vrow = zg_ref[1, b:b + 1, :]
        s_new = jnp.exp(g) * s0_ref[b] + kk * vrow
        sout_ref[b] = s_new
        rows.append(jnp.sum(qt * s_new, axis=0, keepdims=True))
    o = jnp.concatenate(rows, axis=0)
    y = o * lax.rsqrt(jnp.mean(o * o, axis=-1, keepdims=True) + EPS) * on_ref[...]
    o_ref[...] = (y * _silu(zg_ref[2])).astype(o_ref.dtype)


def hgrn_step(z, lb, o_norm, s0, n_heads):
    DB, D4 = z.shape
    D = D4 // 4
    hd = D // n_heads
    z4 = z.reshape(DB, 4, D)
    zcols = jnp.transpose(z4[:, :2, :], (1, 2, 0))
    zrows = jnp.transpose(z4[:, 1:, :], (1, 0, 2))
    sspec = pl.BlockSpec((DB, None, hd, hd), lambda h: (0, h, 0, 0))
    return pl.pallas_call(
        functools.partial(_hgrn_step_kernel, n_seq=DB),
        grid=(n_heads,),
        in_specs=[pl.BlockSpec((2, hd, DB), lambda h: (0, h, 0)),
                  pl.BlockSpec((3, DB, hd), lambda h: (0, 0, h)),
                  pl.BlockSpec((hd, 1), lambda h: (h, 0)),
                  pl.BlockSpec((1, hd), lambda h: (0, 0)),
                  sspec],
        out_specs=(pl.BlockSpec((DB, hd), lambda h: (0, h)), sspec),
        out_shape=(jax.ShapeDtypeStruct((DB, D), BF16), jax.ShapeDtypeStruct(s0.shape, s0.dtype)),
        compiler_params=_params(("parallel",)),
        name="hgrn_step",
    )(zcols, zrows, lb.reshape(D, 1), o_norm, s0)


def _router_kernel(h_ref, w_ref, b_ref, ri_ref, rg_ref, cnt_ref, carry_ref):
    @pl.when(pl.program_id(0) == 0)
    def _():
        carry_ref[...] = jnp.zeros_like(carry_ref)

    hb = h_ref[...].astype(BF16)
    tm = hb.shape[0]
    lane = lax.broadcasted_iota(I32, (tm, V7X_LANES), 1)
    lane_f = lane.astype(F32)
    valid = lane < N_GROUPS
    sc = [_sigmoid(jnp.dot(hb, w_ref[j], preferred_element_type=F32)) for j in range(EXPERTS_PER_GROUP)]
    sel = [jnp.where(valid, sc[j] + b_ref[j], NEG_INF) for j in range(EXPERTS_PER_GROUP)]
    gs = None
    for i in range(EXPERTS_PER_GROUP):
        for j in range(i + 1, EXPERTS_PER_GROUP):
            pair = sel[i] + sel[j]
            gs = pair if gs is None else jnp.maximum(gs, pair)
    gmax = jnp.max(gs, axis=-1, keepdims=True)
    gidx = jnp.min(jnp.where(gs == gmax, lane_f, float(V7X_LANES)), axis=-1, keepdims=True)
    in_g = lane_f == gidx
    pick = lambda x: jnp.sum(jnp.where(in_g, x, 0.0), axis=-1, keepdims=True)
    cand = [pick(sel[j]) for j in range(EXPERTS_PER_GROUP)]
    csc = [pick(sc[j]) for j in range(EXPERTS_PER_GROUP)]

    def first_argmax(vals):
        best, idx = vals[0], jnp.zeros_like(vals[0])
        for j in range(1, len(vals)):
            upd = vals[j] > best
            best = jnp.where(upd, vals[j], best)
            idx = jnp.where(upd, float(j), idx)
        return idx

    i1 = first_argmax(cand)
    i2 = first_argmax([jnp.where(i1 == float(j), NEG_INF, cand[j]) for j in range(EXPERTS_PER_GROUP)])
    at = lambda idx: sum(jnp.where(idx == float(j), csc[j], 0.0) for j in range(EXPERTS_PER_GROUP))
    w1, w2 = at(i1), at(i2)
    wsum = w1 + w2
    e1 = gidx * float(EXPERTS_PER_GROUP) + i1
    e2 = gidx * float(EXPERTS_PER_GROUP) + i2

    hit1 = lane_f == e1
    hit2 = lane_f == e2
    onehot = jnp.where(hit1 | hit2, 1.0, 0.0)
    lower = _iota01((tm, tm), lambda r, c: c < r)
    prior = jnp.dot(lower, onehot.astype(BF16), preferred_element_type=F32) + carry_ref[...]
    r1 = jnp.sum(jnp.where(hit1, prior, 0.0), axis=-1, keepdims=True)
    r2 = jnp.sum(jnp.where(hit2, prior, 0.0), axis=-1, keepdims=True)
    total = carry_ref[...] + jnp.sum(onehot, axis=0, keepdims=True)
    carry_ref[...] = total
    cnt_ref[...] = total.astype(I32)

    ints = jnp.where(lane == 0, e1, jnp.where(lane == 1, e2, jnp.where(lane == 2, r1, jnp.where(lane == 3, r2, 0.0))))
    ri_ref[...] = ints.astype(I32)
    rg_ref[...] = jnp.where(lane == 0, w1 / wsum, jnp.where(lane == 1, w2 / wsum, 0.0))


def moe_route(h, router_w, router_b, tm=256):
    T, D = h.shape
    tm = min(tm, T)
    pad = V7X_LANES - N_GROUPS
    wg = router_w.reshape(D, N_GROUPS, EXPERTS_PER_GROUP).transpose(2, 0, 1)
    wg = jnp.pad(wg, ((0, 0), (0, 0), (0, pad))).astype(BF16)
    bg = jnp.pad(router_b.astype(F32).reshape(N_GROUPS, EXPERTS_PER_GROUP).T, ((0, 0), (0, pad)))
    bg = bg.reshape(EXPERTS_PER_GROUP, 1, V7X_LANES)
    row = pl.BlockSpec((tm, V7X_LANES), lambda i: (i, 0))
    ri, rg, cnt = pl.pallas_call(
        _router_kernel,
        grid=(T // tm,),
        in_specs=[pl.BlockSpec((tm, D), lambda i: (i, 0)),
                  pl.BlockSpec((EXPERTS_PER_GROUP, D, V7X_LANES), lambda i: (0, 0, 0)),
                  pl.BlockSpec((EXPERTS_PER_GROUP, 1, V7X_LANES), lambda i: (0, 0, 0))],
        out_specs=(row, row, pl.BlockSpec((1, V7X_LANES), lambda i: (0, 0))),
        out_shape=(jax.ShapeDtypeStruct((T, V7X_LANES), I32), jax.ShapeDtypeStruct((T, V7X_LANES), F32),
                   jax.ShapeDtypeStruct((1, V7X_LANES), I32)),
        scratch_shapes=[pltpu.VMEM((1, V7X_LANES), F32)],
        compiler_params=_params(("arbitrary",)),
        name="moe_route",
    )(h, wg, bg)
    return ri[:, 0:2], ri[:, 2:4], rg, cnt[0, :N_EXPERTS]


def _dispatch_kernel(dest_ref, h_ref, init_ref, xb_ref, sem, *, tm):
    del init_ref
    base = pl.program_id(0) * tm

    def row_copy(r, k):
        d = dest_ref[(base + r) * TOP_K + k]
        return pltpu.make_async_copy(h_ref.at[pl.ds(r, 1)], xb_ref.at[pl.ds(d, 1)], sem)

    def start(r, _):
        for k in range(TOP_K):
            row_copy(r, k).start()
        return 0

    def wait(r, _):
        for k in range(TOP_K):
            row_copy(r, k).wait()
        return 0

    lax.fori_loop(0, tm, start, 0)
    lax.fori_loop(0, tm, wait, 0)


def moe_dispatch(h, dest, cap, tm=256):
    T, D = h.shape
    tm = min(tm, T)
    grid_spec = pltpu.PrefetchScalarGridSpec(
        num_scalar_prefetch=1,
        grid=(T // tm,),
        in_specs=[pl.BlockSpec((tm, D), lambda i, d: (i, 0)),
                  pl.BlockSpec(memory_space=pl.ANY)],
        out_specs=pl.BlockSpec(memory_space=pl.ANY),
        scratch_shapes=[pltpu.SemaphoreType.DMA(())],
    )
    return pl.pallas_call(
        functools.partial(_dispatch_kernel, tm=tm),
        grid_spec=grid_spec,
        out_shape=jax.ShapeDtypeStruct((cap, D), F32),
        input_output_aliases={2: 0},
        compiler_params=_params(("arbitrary",)),
        name="moe_dispatch",
    )(dest, h, jnp.zeros((cap, D), F32))


def _moe_up_kernel(sb_ref, sc_ref, ib_ref, ic_ref, ie_ref, first_ref, used_ref, ne_ref, nc_ref,
                   x_ref, wg_hbm, wu_hbm, a_ref, stage_ref, wbf_ref, sem, *, layer, ce):
    s = pl.program_id(0)

    def weight_copies(e, c):
        col = pl.multiple_of(c * ce, ce)
        return (pltpu.make_async_copy(wg_hbm.at[layer, e, :, pl.ds(col, ce)], stage_ref.at[0], sem.at[0]),
                pltpu.make_async_copy(wu_hbm.at[layer, e, :, pl.ds(col, ce)], stage_ref.at[1], sem.at[1]))

    @pl.when(s == 0)
    def _():
        for cp in weight_copies(ie_ref[0], ic_ref[0]):
            cp.start()

    @pl.when(used_ref[s] == 1)
    def _():
        @pl.when(first_ref[s] == 1)
        def _():
            for cp in weight_copies(ie_ref[s], ic_ref[s]):
                cp.wait()
            for w in range(2):
                for r in range(0, stage_ref.shape[1], CAST_ROWS):
                    wbf_ref[w, r:r + CAST_ROWS] = stage_ref[w, r:r + CAST_ROWS].astype(BF16)

            @pl.when(ne_ref[s] >= 0)
            def _():
                for cp in weight_copies(ne_ref[s], nc_ref[s]):
                    cp.start()

        xb = x_ref[...].astype(BF16)
        g = jnp.dot(xb, wbf_ref[0], preferred_element_type=F32)
        u = jnp.dot(xb, wbf_ref[1], preferred_element_type=F32)
        a_ref[...] = (_silu(g) * u).astype(BF16)

    @pl.when(used_ref[s] == 0)
    def _():
        a_ref[...] = jnp.zeros_like(a_ref)


def moe_up(xb, tables, w_gate, w_up, layer, blk, ce=512):
    cap, D = xb.shape
    DE = w_gate.shape[3]
    ce = min(ce, DE)
    nsteps = tables[0].shape[0]
    n_tab = len(tables)
    grid_spec = pltpu.PrefetchScalarGridSpec(
        num_scalar_prefetch=n_tab,
        grid=(nsteps,),
        in_specs=[pl.BlockSpec((blk, D), lambda s, *t: (t[2][s], 0)),
                  pl.BlockSpec(memory_space=pl.ANY), pl.BlockSpec(memory_space=pl.ANY)],
        out_specs=pl.BlockSpec((blk, ce), lambda s, *t: (t[0][s], t[1][s])),
        scratch_shapes=[pltpu.VMEM((2, D, ce), F32), pltpu.VMEM((2, D, ce), BF16),
                        pltpu.SemaphoreType.DMA((2,))],
    )
    return pl.pallas_call(
        functools.partial(_moe_up_kernel, layer=layer, ce=ce),
        grid_spec=grid_spec,
        out_shape=jax.ShapeDtypeStruct((cap, DE), BF16),
        compiler_params=_params(("arbitrary",)),
        name="moe_up",
    )(*tables, xb, w_gate, w_up)


def _moe_down_kernel(be_ref, ib_ref, first_ref, nu_ref, ne_ref,
                     a_ref, wd_hbm, y_ref, stage_ref, wdb_ref, sem, *, layer):
    i = pl.program_id(0)

    def weight_copy(e):
        return pltpu.make_async_copy(wd_hbm.at[layer, e], stage_ref, sem)

    @pl.when(i == 0)
    def _():
        weight_copy(be_ref[0]).start()

    @pl.when(i < nu_ref[0])
    def _():
        @pl.when(first_ref[i] == 1)
        def _():
            weight_copy(be_ref[i]).wait()
            for r in range(0, stage_ref.shape[0], CAST_ROWS):
                wdb_ref[r:r + CAST_ROWS] = stage_ref[r:r + CAST_ROWS].astype(BF16)

            @pl.when(ne_ref[i] >= 0)
            def _():
                weight_copy(ne_ref[i]).start()

        y_ref[...] = jnp.dot(a_ref[...], wdb_ref[...], preferred_element_type=F32)

    @pl.when(i >= nu_ref[0])
    def _():
        y_ref[...] = jnp.zeros_like(y_ref)


def moe_down(a, tables, w_down, layer, blk):
    cap, DE = a.shape
    D = w_down.shape[3]
    grid_spec = pltpu.PrefetchScalarGridSpec(
        num_scalar_prefetch=len(tables),
        grid=(cap // blk,),
        in_specs=[pl.BlockSpec((blk, DE), lambda i, *t: (t[1][i], 0)),
                  pl.BlockSpec(memory_space=pl.ANY)],
        out_specs=pl.BlockSpec((blk, D), lambda i, *t: (i, 0)),
        scratch_shapes=[pltpu.VMEM((DE, D), F32), pltpu.VMEM((DE, D), BF16), pltpu.SemaphoreType.DMA(())],
    )
    return pl.pallas_call(
        functools.partial(_moe_down_kernel, layer=layer),
        grid_spec=grid_spec,
        out_shape=jax.ShapeDtypeStruct((cap, D), F32),
        compiler_params=_params(("arbitrary",)),
        name="moe_down",
    )(*tables, a, w_down)


def _next_group(first, used, key_arrays):
    grp = jnp.cumsum(first * used) - 1
    n_groups = jnp.sum(first * used)
    nxt = jnp.minimum(grp + 1, n_groups - 1)
    has_next = (grp + 1 < n_groups) & (used == 1)
    outs = []
    for key in key_arrays:
        per_group = jnp.zeros_like(key).at[jnp.where(used == 1, grp, key.shape[0])].set(key, mode="drop")
        outs.append(jnp.where(has_next, per_group[nxt], -1).astype(I32))
    return outs


def _moe_tables(counts, blk, nblk, nch):
    nb = (counts + blk - 1) // blk
    bend = jnp.cumsum(nb)
    bstart = bend - nb
    nu = bend[-1]
    blocks = jnp.arange(nblk, dtype=I32)
    be = jnp.minimum(jnp.searchsorted(bend, blocks, side="right"), N_EXPERTS - 1).astype(I32)
    used_b = blocks < nu
    ib2 = jnp.minimum(blocks, nu - 1)
    be2 = be[ib2]
    first2 = ((blocks == 0) | (be2 != jnp.roll(be2, 1))).astype(I32)
    (ne2,) = _next_group(first2, used_b.astype(I32), (be2,))
    down = (be2, ib2, first2, nu.reshape(1).astype(I32), ne2)
    start_b = jnp.where(used_b, bstart[be], blocks)
    nb_b = jnp.where(used_b, nb[be], 1)
    chunks = jnp.arange(nch, dtype=I32)
    pos = start_b[:, None] * nch + chunks[None, :] * nb_b[:, None] + (blocks - start_b)[:, None]
    nsteps = nblk * nch
    sb = jnp.zeros((nsteps,), I32).at[pos.reshape(-1)].set(jnp.repeat(blocks, nch))
    sc = jnp.zeros((nsteps,), I32).at[pos.reshape(-1)].set(jnp.tile(chunks, nblk))
    steps = jnp.arange(nsteps, dtype=I32)
    used = (steps < nu * nch).astype(I32)
    clamp = jnp.minimum(steps, nu * nch - 1)
    ib, ic = sb[clamp], sc[clamp]
    ie = be[ib]
    first = ((steps == 0) | (ie != jnp.roll(ie, 1)) | (ic != jnp.roll(ic, 1))).astype(I32)
    ne, nc = _next_group(first, used, (ie, ic))
    up = (sb, sc, ib, ic, ie, first, used, ne, nc)
    return bstart * blk, up, down


def _combine_kernel(dest_ref, yb_ref, x_ref, gate_ref, rg_ref, o_ref, ybuf_ref, sem, *, tm):
    i = pl.program_id(0)
    slot = i % 2

    def row_copy(step, sl, r, k):
        d = dest_ref[(step * tm + r) * TOP_K + k]
        return pltpu.make_async_copy(yb_ref.at[pl.ds(d, 1)], ybuf_ref.at[sl, k, pl.ds(r, 1)], sem.at[sl])

    def start_all(step, sl):
        def body(r, _):
            for k in range(TOP_K):
                row_copy(step, sl, r, k).start()
            return 0
        lax.fori_loop(0, tm, body, 0)

    def wait_all(step, sl):
        def body(r, _):
            for k in range(TOP_K):
                row_copy(step, sl, r, k).wait()
            return 0
        lax.fori_loop(0, tm, body, 0)

    @pl.when(i == 0)
    def _():
        start_all(0, 0)

    @pl.when(i + 1 < pl.num_programs(0))
    def _():
        start_all(i + 1, 1 - slot)

    wait_all(i, slot)
    rg = rg_ref[...]
    y = ybuf_ref[slot, 0] * rg[:, 0:1] + ybuf_ref[slot, 1] * rg[:, 1:2]
    o_ref[...] = x_ref[...] + gate_ref[...] * y


def moe_combine(yb, dest, x, gate, rg, rows_per_batch, tm=256):
    T, D = x.shape
    tm = min(tm, T)
    if gate.shape[1] == 1:
        rpt = rows_per_batch // tm
        gspec = pl.BlockSpec((None, 1, D), lambda i, d: (i // rpt, 0, 0))
    else:
        gspec = pl.BlockSpec((None, tm, D), lambda i, d: (0, i, 0))
    grid_spec = pltpu.PrefetchScalarGridSpec(
        num_scalar_prefetch=1,
        grid=(T // tm,),
        in_specs=[pl.BlockSpec(memory_space=pl.ANY),
                  pl.BlockSpec((tm, D), lambda i, d: (i, 0)),
                  gspec,
                  pl.BlockSpec((tm, V7X_LANES), lambda i, d: (i, 0))],
        out_specs=pl.BlockSpec((tm, D), lambda i, d: (i, 0)),
        scratch_shapes=[pltpu.VMEM((2, TOP_K, tm, D), F32), pltpu.SemaphoreType.DMA((2,))],
    )
    return pl.pallas_call(
        functools.partial(_combine_kernel, tm=tm),
        grid_spec=grid_spec,
        out_shape=jax.ShapeDtypeStruct((T, D), F32),
        compiler_params=_params(("arbitrary",)),
        name="moe_combine",
    )(dest, yb, x, gate, rg)


def moe_sublayer(h, x, gate, rows_per_batch, router_w, router_b, w_gate, w_up, w_down, layer, blk, ce=512):
    T, D = h.shape
    DE = w_gate.shape[3]
    ce = min(ce, DE)
    e_idx, rank, rg, counts = moe_route(h, router_w, router_b, tm=min(256, T))
    A = T * TOP_K
    cap = -(-(A + N_EXPERTS * (blk - 1)) // blk) * blk
    pad_start, up_tables, down_tables = _moe_tables(counts, blk, cap // blk, DE // ce)
    dest = (pad_start[e_idx] + rank).reshape(-1).astype(I32)
    xb = moe_dispatch(h, dest, cap, tm=min(256, T))
    a = moe_up(xb, up_tables, w_gate, w_up, layer, blk, ce)
    yb = moe_down(a, down_tables, w_down, layer, blk)
    return moe_combine(yb, dest, x, gate, rg, rows_per_batch, tm=min(256, T))


def kernel(x_prompt, x_sample, c_prompt, c_sample, cache_k, cache_v, cache_logf, state_hgrn, page_table,
           ada_w, ada_b, norm_g, att_w_in, att_b_f, att_q_norm, att_k_norm, att_w_o,
           rec_w_in, rec_lb_logits, rec_o_norm, rec_w_o, router_w, router_b,
           moe_w_gate, moe_w_up, moe_w_down):
    B, L, D = x_prompt.shape
    DB, LS, _ = x_sample.shape
    depth = ada_w.shape[0]
    H = ATT_HEADS
    hd = D // H
    T = B * L
    TS = DB * LS
    assert LS == 1
    P = cache_k.shape[2]

    p_lb = jax.nn.softmax(rec_lb_logits.astype(F32), axis=0)
    lb_table = jnp.cumsum(p_lb, axis=0) - p_lb[0]

    n_c = B + DB
    r_c = -(-n_c // V7X_SUBLANES) * V7X_SUBLANES
    c_all = jnp.pad(jnp.concatenate([c_prompt, c_sample], axis=0), ((0, r_c - n_c), (0, 0)))
    mod = ada_modulation(c_all, ada_w.reshape(depth * 2, D, 3 * D), ada_b.reshape(depth * 2, 1, 3 * D))
    mod = mod.reshape(depth, 2, r_c, 3, D)

    def mods(layer, sub):
        m = mod[layer, sub]
        shift, scale, gate = m[:, 0], m[:, 1], m[:, 2]
        pr = lambda a: a[:B].reshape(B, 1, D)
        sa = lambda a: a[B:n_c].reshape(1, DB, D)
        return (pr(shift), pr(scale), pr(gate)), (sa(shift), sa(scale), sa(gate))

    xp = x_prompt.reshape(T, D)
    xs = x_sample.reshape(TS, D)
    outs = {n: [] for n in ("kp", "vp", "lfp", "ks", "vs", "lfs", "sp", "ss")}
    blk_p = max(16, min(256, T * TOP_K // N_EXPERTS))
    blk_s = max(16, min(256, TS * TOP_K // N_EXPERTS))
    n_pool = cache_k.shape[1]
    ck = cache_k.reshape(cache_k.shape[0], n_pool, P * H, hd)
    cv = cache_v.reshape(cache_v.shape[0], n_pool, P * H, hd)

    for layer in range(depth):
        j = layer // 2
        (shp, scp, gp), (shs, scs, gs) = mods(layer, 0)
        g0 = norm_g[layer, 0].reshape(1, D)
        hp = norm_modulate(xp.reshape(B, L, D), g0, scp, shp).reshape(T, D)
        hs = norm_modulate(xs.reshape(1, TS, D), g0, scs, shs).reshape(TS, D)
        if layer % 2 == 0:
            w_in = att_w_in
            w_f = att_w_in[j][:, 3 * D:].reshape(1, D, H)
            b_f = att_b_f[j].reshape(1, H)
            qg = att_q_norm[j].reshape(1, hd)
            kg = att_k_norm[j].reshape(1, hd)
            q_bf = project(hp, w_in, layer=j, mode="headnorm_bf", col_off=0, n_out=D, extra=(qg,))
            k_f, k_bf = project(hp, w_in, layer=j, mode="headnorm", col_off=D, n_out=D, extra=(kg,))
            v_f, v_bf = project(hp, w_in, layer=j, mode="dual", col_off=2 * D, n_out=D)
            lf = project(hp, w_f, mode="logsig", extra=(b_f,)).reshape(B, L, H)
            cum = seq_cumsum(lf)
            cum_t = jnp.transpose(cum, (0, 2, 1)).reshape(B, H, 1, L)
            op = fox_prompt_attention(q_bf.reshape(B, L, D), k_bf.reshape(B, L, D), v_bf.reshape(B, L, D),
                                      cum, cum_t, H)
            xp = project(op.reshape(T, D), att_w_o, layer=j, mode="resid", extra=(xp, gp), rows_per_batch=L)
            outs["kp"].append(k_f.reshape(B, L, H, hd))
            outs["vp"].append(v_f.reshape(B, L, H, hd))
            outs["lfp"].append(lf)
            qs, _ = project(hs, w_in, layer=j, mode="headnorm", col_off=0, n_out=D, extra=(qg,))
            ks, _ = project(hs, w_in, layer=j, mode="headnorm", col_off=D, n_out=D, extra=(kg,))
            vs = project(hs, w_in, layer=j, mode="plain", col_off=2 * D, n_out=D)
            lfs = project(hs, w_f, mode="logsig", extra=(b_f,))
            bias = fox_sample_bias(page_table, cache_logf, j, lfs.reshape(DB, 1, H))
            osm = fox_sample_attention(page_table, qs.reshape(DB, H, hd), ks.reshape(DB, H, hd),
                                       vs.reshape(DB, H, hd), ck, cv, j,
                                       bias.reshape(DB, bias.shape[1], 1, P * H))
            xs = project(osm.reshape(TS, D), att_w_o, layer=j, mode="resid", extra=(xs, gs))
            outs["ks"].append(ks.reshape(DB, LS, H, hd))
            outs["vs"].append(vs.reshape(DB, LS, H, hd))
            outs["lfs"].append(lfs.reshape(DB, LS, H))
        else:
            lb = lb_table[layer].reshape(1, D)
            on = rec_o_norm[j].reshape(1, hd)
            zp = project(hp, rec_w_in, layer=j, mode="plain")
            s0p = jnp.zeros((B, REC_HEADS, hd, hd), state_hgrn.dtype)
            op, sp = hgrn_prompt(zp.reshape(B, L, 4 * D), lb, on, s0p, REC_HEADS)
            xp = project(op.reshape(T, D), rec_w_o, layer=j, mode="resid", extra=(xp, gp), rows_per_batch=L)
            zs = project(hs, rec_w_in, layer=j, mode="plain")
            osm, ss = hgrn_step(zs, lb_table[layer], on, state_hgrn[j], REC_HEADS)
            xs = project(osm, rec_w_o, layer=j, mode="resid", extra=(xs, gs))
            outs["sp"].append(sp)
            outs["ss"].append(ss)

        (shp, scp, gp), (shs, scs, gs) = mods(layer, 1)
        g1 = norm_g[layer, 1].reshape(1, D)
        hp = norm_modulate(xp.reshape(B, L, D), g1, scp, shp, out_dtype=F32).reshape(T, D)
        hs = norm_modulate(xs.reshape(1, TS, D), g1, scs, shs, out_dtype=F32).reshape(TS, D)
        moe_w = (router_w, router_b, moe_w_gate, moe_w_up, moe_w_down, layer)
        xp = moe_sublayer(hp, xp, gp, L, *moe_w, blk=blk_p)
        xs = moe_sublayer(hs, xs, gs, None, *moe_w, blk=blk_s)

    st = lambda n: jnp.stack(outs[n])
    return (xp.reshape(B, L, D), xs.reshape(DB, LS, D), st("kp"), st("vp"), st("lfp"), st("ks"), st("vs"),
            st("lfs"), st("sp"), st("ss"))
```
